```python
import math
import jax, jax.numpy as jnp
from jax import lax
import numpy as np

D_MODEL = 1024
BATCH = 4
SEQ = 4096
DEPTH = 4
DEC_BATCH = 16
DEC_SEQ = 64
PAST_LEN = 1024

CHUNK = 64
D_MIX = D_MODEL
D_A = D_MIX // 2
D_B = D_MIX - D_A
HEAD_DIM = 64
N_HEADS_A = D_A // HEAD_DIM
N_KV_HEADS = 2
GQA_GROUP = N_HEADS_A // N_KV_HEADS
D_KV = N_KV_HEADS * HEAD_DIM
ROT_DIM = HEAD_DIM // 4
N_IDX_HEADS = 8
IDX_DIM = 32
IDX_ROT = IDX_DIM // 4
TOPK_MAX = 256
QBLK = 128
N_GROUPS_B = 8
GROUP_DIM_B = D_B // N_GROUPS_B
GMLP_CHUNK = 128
D_FF = 4 * D_MODEL
ROPE_THETA = 500000.0
EPS = 1e-6
N_IN = D_A + 2 * D_KV + N_IDX_HEADS * IDX_DIM + IDX_DIM + N_IDX_HEADS + 2 * D_B

kernel_name = 'hybrid_dsa_gmlp_stream_step'


def rmsnorm(x, g):
    xf = x.astype(jnp.float32)
    y = xf * lax.rsqrt(jnp.mean(xf * xf, axis=-1, keepdims=True) + EPS)
    return (y * g.astype(jnp.float32)).astype(x.dtype)


def rope(x, pos, rot):
    half = rot // 2
    inv = ROPE_THETA ** (-jnp.arange(half, dtype=jnp.float32) * (2.0 / rot))
    ang = pos.astype(jnp.float32)[:, None] * inv[None, :]
    cos = jnp.cos(ang)[None, :, None, :]
    sin = jnp.sin(ang)[None, :, None, :]
    xf = x.astype(jnp.float32)
    x1 = xf[..., :half]
    x2 = xf[..., half:rot]
    out = jnp.concatenate([x1 * cos - x2 * sin, x2 * cos + x1 * sin, xf[..., rot:]], axis=-1)
    return out.astype(x.dtype)


def chunk_mask(n, dtype):
    c = jnp.arange(n) // CHUNK
    return (c[None, :] <= c[:, None]).astype(dtype)


def project(xn, w_in, pos):
    B, T, _ = xn.shape
    z = xn @ w_in
    sizes = (D_A, D_KV, D_KV, N_IDX_HEADS * IDX_DIM, IDX_DIM, N_IDX_HEADS, D_B, D_B)
    parts = []
    off = 0
    for s in sizes:
        parts.append(z[..., off:off + s])
        off += s
    q, k, v, qi, ki, wi, u, gv = parts
    q = rope(q.reshape(B, T, N_HEADS_A, HEAD_DIM), pos, ROT_DIM)
    k = rope(k.reshape(B, T, N_KV_HEADS, HEAD_DIM), pos, ROT_DIM)
    v = v.reshape(B, T, N_KV_HEADS, HEAD_DIM)
    qi = rope(qi.reshape(B, T, N_IDX_HEADS, IDX_DIM), pos, IDX_ROT)
    ki = rope(ki.reshape(B, T, 1, IDX_DIM), pos, IDX_ROT)[:, :, 0]
    return q, k, v, qi, ki, wi, u, gv


def dsa_attend(q, qi, wi, pos_q, k, v, ki, pos_k, topk):
    B, Tq = q.shape[:2]
    rel = jax.nn.relu(jnp.einsum('bthd,bsd->bths', qi.astype(jnp.float32), ki.astype(jnp.float32)))
    score = jnp.einsum('bths,bth->bts', rel, wi.astype(jnp.float32)) * (IDX_DIM ** -0.5 * N_IDX_HEADS ** -0.5)
    admissible = (pos_k[None, :] // CHUNK) <= (pos_q[:, None] // CHUNK)
    score = jnp.where(admissible[None], score, -jnp.inf)
    top_val, top_idx = lax.top_k(score, topk)
    valid = jnp.isfinite(top_val)
    take_rows = jax.vmap(lambda a, idx: a[idx])
    k_sel = take_rows(k, top_idx)
    v_sel = take_rows(v, top_idx)
    qg = q.reshape(B, Tq, N_KV_HEADS, GQA_GROUP, HEAD_DIM)
    logits = jnp.einsum('btkgd,btskd->btkgs', qg, k_sel).astype(jnp.float32) * (HEAD_DIM ** -0.5)
    logits = jnp.where(valid[:, :, None, None, :], logits, -jnp.inf)
    p = jax.nn.softmax(logits, axis=-1).astype(v.dtype)
    o = jnp.einsum('btkgs,btskd->btkgd', p, v_sel)
    return o.reshape(B, Tq, D_A)


def dsa_prompt(q, qi, wi, k, v, ki, pos):
    B, T = q.shape[:2]
    nb = T // QBLK
    topk = min(TOPK_MAX, T // 4)

    def blocks(a):
        return jnp.swapaxes(a.reshape((B, nb, QBLK) + a.shape[2:]), 0, 1)

    out = lax.map(lambda args: dsa_attend(args[0], args[1], args[2], args[3], k, v, ki, pos, topk),
                  (blocks(q), blocks(qi), blocks(wi), pos.reshape(nb, QBLK)))
    return jnp.swapaxes(out, 0, 1).reshape(B, T, D_A)


def spatial_prompt(gvn, w_s, b_s):
    B, T, _ = gvn.shape
    n = T // GMLP_CHUNK
    vg = gvn.reshape(B, n, GMLP_CHUNK, N_GROUPS_B, GROUP_DIM_B)
    w = w_s * chunk_mask(GMLP_CHUNK, w_s.dtype)[None]
    mix = jnp.einsum('gij,bnjgc->bnigc', w, vg) + b_s.T[None, None, :, :, None]
    return mix.reshape(B, T, D_B)


def spatial_sample(gvn, w_s, b_s):
    B, Ts, _ = gvn.shape
    vg = gvn.reshape(B, Ts, N_GROUPS_B, GROUP_DIM_B)
    w = w_s[:, :Ts, :Ts] * chunk_mask(Ts, w_s.dtype)[None]
    mix = jnp.einsum('gij,bjgc->bigc', w, vg) + b_s[:, :Ts].T[None, :, :, None]
    return mix.reshape(B, Ts, D_B)


def sq_relu_ffn(x, w1, w2):
    h = jax.nn.relu(x @ w1)
    return (h * h) @ w2


def setup_inputs(seed: int = 0) -> dict:
    key = jax.random.key(seed)
    ks = jax.random.split(key, 16)
    f32 = jnp.float32
    nrm = lambda k, shape, scale: jax.random.normal(k, shape, f32) * scale
    return {
        'x_prompt': nrm(ks[0], (BATCH, SEQ, D_MODEL), 1.0),
        'x_sample': nrm(ks[1], (DEC_BATCH, DEC_SEQ, D_MODEL), 1.0),
        'cache_k': nrm(ks[2], (DEPTH, DEC_BATCH, PAST_LEN, N_KV_HEADS, HEAD_DIM), 1.0),
        'cache_v': nrm(ks[3], (DEPTH, DEC_BATCH, PAST_LEN, N_KV_HEADS, HEAD_DIM), 1.0),
        'cache_kidx': nrm(ks[4], (DEPTH, DEC_BATCH, PAST_LEN, IDX_DIM), 1.0),
        'norm_mix': 1.0 + nrm(ks[5], (DEPTH, D_MODEL), 0.02),
        'w_in': nrm(ks[6], (DEPTH, D_MODEL, N_IN), D_MODEL ** -0.5),
        'gate_norm': 1.0 + nrm(ks[7], (DEPTH, D_B), 0.02),
        'w_spatial': nrm(ks[8], (DEPTH, N_GROUPS_B, GMLP_CHUNK, GMLP_CHUNK), GMLP_CHUNK ** -0.5),
        'b_spatial': 1.0 + nrm(ks[9], (DEPTH, N_GROUPS_B, GMLP_CHUNK), 0.02),
        'w_out': nrm(ks[10], (DEPTH, D_MIX, D_MODEL), D_MIX ** -0.5),
        'norm_ffn': 1.0 + nrm(ks[11], (DEPTH, D_MODEL), 0.02),
        'w_ff1': nrm(ks[12], (DEPTH, D_MODEL, D_FF), D_MODEL ** -0.5),
        'w_ff2': nrm(ks[13], (DEPTH, D_FF, D_MODEL), D_FF ** -0.5),
        'norm_final': 1.0 + nrm(ks[14], (D_MODEL,), 0.02),
    }


def reference(x_prompt, x_sample, cache_k, cache_v, cache_kidx, norm_mix, w_in, gate_norm,
              w_spatial, b_spatial, w_out, norm_ffn, w_ff1, w_ff2, norm_final):
    T = x_prompt.shape[1]
    Ts = x_sample.shape[1]
    past = cache_k.shape[2]
    pos_p = jnp.arange(T, dtype=jnp.int32)
    pos_s = past + jnp.arange(Ts, dtype=jnp.int32)
    pos_all = jnp.arange(past + Ts, dtype=jnp.int32)
    topk_s = min(TOPK_MAX, (past + Ts) // 4)

    xp = x_prompt
    xs = x_sample
    kp_l, vp_l, kip_l, ks_l, vs_l, kis_l, gvs_l = [], [], [], [], [], [], []
    for l in range(DEPTH):
        xn = rmsnorm(xp, norm_mix[l])
        q, k, v, qi, ki, wi, u, gv = project(xn, w_in[l], pos_p)
        a = dsa_prompt(q, qi, wi, k, v, ki, pos_p)
        gvn = rmsnorm(jax.nn.gelu(gv), gate_norm[l])
        bmix = jax.nn.gelu(u) * spatial_prompt(gvn, w_spatial[l], b_spatial[l])
        xp = xp + jnp.concatenate([a, bmix], axis=-1) @ w_out[l]
        xp = xp + sq_relu_ffn(rmsnorm(xp, norm_ffn[l]), w_ff1[l], w_ff2[l])
        kp_l.append(k)
        vp_l.append(v)
        kip_l.append(ki)

        xn = rmsnorm(xs, norm_mix[l])
        q, k, v, qi, ki, wi, u, gv = project(xn, w_in[l], pos_s)
        k_all = jnp.concatenate([cache_k[l], k], axis=1)
        v_all = jnp.concatenate([cache_v[l], v], axis=1)
        ki_all = jnp.concatenate([cache_kidx[l], ki], axis=1)
        a = dsa_attend(q, qi, wi, pos_s, k_all, v_all, ki_all, pos_all, topk_s)
        gvn = rmsnorm(jax.nn.gelu(gv), gate_norm[l])
        bmix = jax.nn.gelu(u) * spatial_sample(gvn, w_spatial[l], b_spatial[l])
        xs = xs + jnp.concatenate([a, bmix], axis=-1) @ w_out[l]
        xs = xs + sq_relu_ffn(rmsnorm(xs, norm_ffn[l]), w_ff1[l], w_ff2[l])
        ks_l.append(k)
        vs_l.append(v)
        kis_l.append(ki)
        gvs_l.append(gvn)

    y_prompt = rmsnorm(xp, norm_final)
    y_sample = rmsnorm(xs, norm_final)
    k_prompt = jnp.stack(kp_l)
    v_prompt = jnp.stack(vp_l)
    kidx_prompt = jnp.stack(kip_l)
    k_sample = jnp.stack(ks_l)
    v_sample = jnp.stack(vs_l)
    kidx_sample = jnp.stack(kis_l)
    gv_sample = jnp.stack(gvs_l)
    return (y_prompt, y_sample, k_prompt, v_prompt, kidx_prompt, k_sample, v_sample, kidx_sample, gv_sample)
```

```python
import functools

import numpy as np
import jax
import jax.numpy as jnp
from jax import lax
from jax.experimental import pallas as pl
from jax.experimental.pallas import tpu as pltpu

D_MODEL = 1024
CHUNK = 64
D_A = 512
D_B = 512
HEAD_DIM = 64
N_HEADS_A = 8
N_KV_HEADS = 2
GQA_GROUP = 4
D_KV = 128
ROT_DIM = 16
N_IDX_HEADS = 8
IDX_DIM = 32
IDX_ROT = 8
TOPK_MAX = 256
N_GROUPS_B = 8
GROUP_DIM_B = 64
GMLP_CHUNK = 128
D_FF = 4096
ROPE_THETA = 500000.0
EPS = 1e-6

LANES = 128
N_IN_PAD = 2176
TM = 512
TQ = 128
KT = 256
VMEM_LIMIT = 56 * 1024 * 1024

INT_MIN = np.int32(-2147483648)
NEG_BIG = -1e30

_Q0, _K0, _V0, _QI0, _KW0, _U0, _GV0 = 0, 512, 640, 768, 1024, 1152, 1664


def _gelu(x):
    return x * (0.5 * (1.0 + jnp.tanh(0.7978845608028654 * (x + 0.044715 * (x * x * x)))))


def _rope128(x, c, s_lo, s_hi, half):
    return x * c + pltpu.roll(x, LANES - half, 1) * s_lo + pltpu.roll(x, half, 1) * s_hi


def _proj_kernel(x_ref, g_ref, w_ref, tab_ref, gn_ref,
                 q_ref, k_ref, v_ref, kw_ref, qi_ref, gu_ref, gvn_ref):
    x = x_ref[...]
    ms = jnp.mean(x * x, axis=-1, keepdims=True)
    xn = (x * lax.rsqrt(ms + EPS) * g_ref[...]).astype(jnp.bfloat16)
    z = jnp.dot(xn, w_ref[...], preferred_element_type=jnp.float32)

    def tab(j):
        return tab_ref[:, j * LANES:(j + 1) * LANES]

    cq, sq_lo, sq_hi = tab(0), tab(1), tab(2)
    ci, si_lo, si_hi = tab(3), tab(4), tab(5)
    ck, sk_lo, sk_hi = tab(6), tab(7), tab(8)

    for j in range(D_A // LANES):
        zq = z[:, _Q0 + j * LANES:_Q0 + (j + 1) * LANES]
        q_ref[:, j * LANES:(j + 1) * LANES] = (
            _rope128(zq, cq, sq_lo, sq_hi, ROT_DIM // 2) * (HEAD_DIM ** -0.5)).astype(q_ref.dtype)
    k_ref[...] = _rope128(z[:, _K0:_K0 + LANES], cq, sq_lo, sq_hi, ROT_DIM // 2)
    v_ref[...] = z[:, _V0:_V0 + LANES]
    for j in range(N_IDX_HEADS * IDX_DIM // LANES):
        zi = z[:, _QI0 + j * LANES:_QI0 + (j + 1) * LANES]
        qi_ref[:, j * LANES:(j + 1) * LANES] = _rope128(zi, ci, si_lo, si_hi, IDX_ROT // 2).astype(qi_ref.dtype)
    kw_ref[...] = _rope128(z[:, _KW0:_KW0 + LANES], ck, sk_lo, sk_hi, IDX_ROT // 2)

    gu_ref[...] = _gelu(z[:, _U0:_U0 + D_B]).astype(gu_ref.dtype)
    gg = _gelu(z[:, _GV0:_GV0 + D_B])
    gms = jnp.mean(gg * gg, axis=-1, keepdims=True)
    gvn_ref[...] = gg * lax.rsqrt(gms + EPS) * gn_ref[...]


def _proj_call(x, g, w, tab, gn, n_prompt_tiles, n_pos_tiles):
    rows = x.shape[0]
    n_tiles = rows // TM

    def row_map(i):
        return (i, 0)

    def const_map(i):
        return (0, 0)

    def tab_map(i):
        return (jnp.where(i < n_prompt_tiles, i % n_pos_tiles, n_pos_tiles), 0)

    f32, bf16 = jnp.float32, jnp.bfloat16
    out_shape = (
        jax.ShapeDtypeStruct((rows, D_A), bf16),
        jax.ShapeDtypeStruct((rows, D_KV), f32),
        jax.ShapeDtypeStruct((rows, D_KV), f32),
        jax.ShapeDtypeStruct((rows, LANES), f32),
        jax.ShapeDtypeStruct((rows, N_IDX_HEADS * IDX_DIM), bf16),
        jax.ShapeDtypeStruct((rows, D_B), bf16),
        jax.ShapeDtypeStruct((rows, D_B), f32),
    )
    return pl.pallas_call(
        _proj_kernel,
        grid=(n_tiles,),
        in_specs=[
            pl.BlockSpec((TM, D_MODEL), row_map),
            pl.BlockSpec((1, D_MODEL), const_map),
            pl.BlockSpec((D_MODEL, N_IN_PAD), const_map),
            pl.BlockSpec((TM, 9 * LANES), tab_map),
            pl.BlockSpec((1, D_B), const_map),
        ],
        out_specs=tuple(pl.BlockSpec((TM, s.shape[1]), row_map) for s in out_shape),
        out_shape=out_shape,
        compiler_params=pltpu.CompilerParams(
            dimension_semantics=("arbitrary",), vmem_limit_bytes=VMEM_LIMIT),
        name="proj",
    )(x, g, w, tab, gn)


def _dsa_kernel(qT_ref, qiT_ref, wi_ref, k_ref, vT_ref, ki_ref, gvn_ref, gu_ref, wm_ref, bias_ref,
                oT_ref, bmix_ref, skey_ref, *, causal, q_pos0, tq_real, n_tiles_static, topk):
    f32, bf16, i32 = jnp.float32, jnp.bfloat16, jnp.int32
    i = pl.program_id(1)
    if causal:
        n_tiles = ((i + 1) * TQ + KT - 1) // KT
        q0 = i * TQ
    else:
        n_tiles = n_tiles_static
        q0 = q_pos0
    lane = lax.broadcasted_iota(i32, (1, TQ), 1)
    qchunk = (q0 + (lane & (tq_real - 1))) // CHUNK

    def score_tile(t, carry):
        ki_t = ki_ref[t]
        acc = jnp.zeros((KT, TQ), f32)
        for hp in range(N_IDX_HEADS // 2):
            r = jnp.dot(ki_t, qiT_ref[:, hp * 2 * TQ:(hp + 1) * 2 * TQ], preferred_element_type=f32)
            r = jnp.maximum(r, 0.0) * wi_ref[:, hp * 2 * TQ:(hp + 1) * 2 * TQ]
            acc = acc + r[:, :TQ] + r[:, TQ:]
        kpos = t * KT + lax.broadcasted_iota(i32, (KT, TQ), 0)
        adm = (kpos // CHUNK) <= qchunk
        bits = pltpu.bitcast(acc, i32)
        key = bits ^ ((bits >> 31) & np.int32(0x7FFFFFFF))
        skey_ref[t] = jnp.where(adm, key, INT_MIN)
        return carry

    lax.fori_loop(0, n_tiles, score_tile, 0)

    def count(pred_fn):
        def body(t, cnt):
            return cnt + jnp.sum(jnp.where(pred_fn(skey_ref[t]), 1, 0).astype(i32), axis=0, keepdims=True)
        return lax.fori_loop(0, n_tiles, body, jnp.zeros((1, TQ), i32))

    def bit_body(j, st):
        t_u, cnt_at = st
        c_u = t_u | jnp.left_shift(jnp.int32(1), 31 - j)
        c_s = c_u ^ INT_MIN
        cnt = count(lambda tile: tile >= c_s)
        ok = cnt >= topk
        return jnp.where(ok, c_u, t_u), jnp.where(ok, cnt, cnt_at)

    t_u, cnt_ge = lax.fori_loop(0, 32, bit_body, (jnp.zeros((1, TQ), i32), jnp.zeros((1, TQ), i32)))
    thr = jnp.maximum(t_u ^ INT_MIN, INT_MIN + 1)

    @pl.when(jnp.max(cnt_ge) > topk)
    def _():
        need = (topk - count(lambda tile: tile > thr)).astype(f32)
        rr = lax.broadcasted_iota(i32, (KT, KT), 0)
        cc = lax.broadcasted_iota(i32, (KT, KT), 1)
        tri = jnp.where(cc <= rr, 1.0, 0.0).astype(bf16)

        def body(t, seen):
            tile = skey_ref[t]
            eq = tile == thr
            eqf = jnp.where(eq, 1.0, 0.0).astype(bf16)
            rank = jnp.dot(tri, eqf, preferred_element_type=f32) + seen
            skey_ref[t] = jnp.where(jnp.logical_and(eq, rank > need), INT_MIN, tile)
            return rank[KT - 1:KT, :]

        lax.fori_loop(0, n_tiles, body, jnp.zeros((1, TQ), f32))

    def att_tile(t, st):
        bias = jnp.where(skey_ref[t] >= thr, 0.0, -jnp.inf)
        bias = jnp.concatenate([bias] * GQA_GROUP, axis=1)
        out = []
        for g in range(N_KV_HEADS):
            m, l, acc = st[g]
            s = jnp.dot(k_ref[g, t], qT_ref[g], preferred_element_type=f32) + bias
            m_new = jnp.maximum(m, jnp.max(s, axis=0, keepdims=True))
            alpha = jnp.exp(m - m_new)
            p = jnp.exp(s - m_new)
            l = alpha * l + jnp.sum(p, axis=0, keepdims=True)
            acc = alpha * acc + jnp.dot(vT_ref[g, t], p.astype(bf16), preferred_element_type=f32)
            out.append((m_new, l, acc))
        return tuple(out)

    init = tuple((jnp.full((1, GQA_GROUP * TQ), NEG_BIG, f32),
                  jnp.zeros((1, GQA_GROUP * TQ), f32),
                  jnp.zeros((HEAD_DIM, GQA_GROUP * TQ), f32)) for _ in range(N_KV_HEADS))
    fin = lax.fori_loop(0, n_tiles, att_tile, init)
    for g in range(N_KV_HEADS):
        _, l, acc = fin[g]
        oT_ref[g] = acc * (1.0 / l)

    lane_c = lax.broadcasted_iota(i32, (GMLP_CHUNK, LANES), 1)
    for p2 in range(N_GROUPS_B // 2):
        xg = gvn_ref[:, p2 * LANES:(p2 + 1) * LANES].astype(bf16)
        both = jnp.dot(wm_ref[p2], xg, preferred_element_type=f32)
        mix = jnp.where(lane_c < GROUP_DIM_B, both[:GMLP_CHUNK], both[GMLP_CHUNK:])
        mix = mix + bias_ref[:, p2 * LANES:(p2 + 1) * LANES]
        gu = gu_ref[:, p2 * LANES:(p2 + 1) * LANES].astype(f32)
        bmix_ref[:, p2 * LANES:(p2 + 1) * LANES] = (gu * mix).astype(bmix_ref.dtype)


def _dsa_call(qT, qiT, wi, k, vT, ki, gvn, gu, wm, bias, *, causal, q_pos0, tq_real, topk):
    nbatch, nblk = qT.shape[:2]
    n_kt = k.shape[2]
    kern = functools.partial(_dsa_kernel, causal=causal, q_pos0=q_pos0, tq_real=tq_real,
                             n_tiles_static=n_kt, topk=topk)
    blk = lambda b, i: (b, i, 0, 0)
    per_b4 = lambda b, i: (b, 0, 0, 0)
    per_b5 = lambda b, i: (b, 0, 0, 0, 0)
    return pl.pallas_call(
        kern,
        grid=(nbatch, nblk),
        in_specs=[
            pl.BlockSpec((None, None, N_KV_HEADS, HEAD_DIM, GQA_GROUP * TQ), lambda b, i: (b, i, 0, 0, 0)),
            pl.BlockSpec((None, None, IDX_DIM, N_IDX_HEADS * TQ), blk),
            pl.BlockSpec((None, None, 1, N_IDX_HEADS * TQ), blk),
            pl.BlockSpec((None, N_KV_HEADS, n_kt, KT, HEAD_DIM), per_b5),
            pl.BlockSpec((None, N_KV_HEADS, n_kt, HEAD_DIM, KT), per_b5),
            pl.BlockSpec((None, n_kt, KT, IDX_DIM), per_b4),
            pl.BlockSpec((None, None, TQ, D_B), blk),
            pl.BlockSpec((None, None, TQ, D_B), blk),
            pl.BlockSpec((N_GROUPS_B // 2, 2 * GMLP_CHUNK, GMLP_CHUNK), lambda b, i: (0, 0, 0)),
            pl.BlockSpec((GMLP_CHUNK, D_B), lambda b, i: (0, 0)),
        ],
        out_specs=(
            pl.BlockSpec((None, None, N_KV_HEADS, HEAD_DIM, GQA_GROUP * TQ), lambda b, i: (b, i, 0, 0, 0)),
            pl.BlockSpec((None, None, TQ, D_B), blk),
        ),
        out_shape=(
            jax.ShapeDtypeStruct((nbatch, nblk, N_KV_HEADS, HEAD_DIM, GQA_GROUP * TQ), jnp.float32),
            jax.ShapeDtypeStruct((nbatch, nblk, TQ, D_B), jnp.bfloat16),
        ),
        scratch_shapes=[pltpu.VMEM((n_kt, KT, TQ), jnp.int32)],
        compiler_params=pltpu.CompilerParams(
            dimension_semantics=("arbitrary", "arbitrary"), vmem_limit_bytes=VMEM_LIMIT),
        name="dsa_prompt" if causal else "dsa_sample",
    )(qT, qiT, wi, k, vT, ki, gvn, gu, wm, bias)


def _ffn_kernel(x_ref, a_ref, bm_ref, woa_ref, wob_ref, g_ref, w1_ref, w2_ref, gf_ref, o_ref, *, final):
    f32 = jnp.float32
    x = x_ref[...]
    x = x + jnp.dot(a_ref[...], woa_ref[...], preferred_element_type=f32)
    x = x + jnp.dot(bm_ref[...], wob_ref[...], preferred_element_type=f32)
    ms = jnp.mean(x * x, axis=-1, keepdims=True)
    xn = (x * lax.rsqrt(ms + EPS) * g_ref[...]).astype(jnp.bfloat16)
    h = jnp.maximum(jnp.dot(xn, w1_ref[...], preferred_element_type=f32), 0.0)
    acc = x + jnp.dot((h * h).astype(jnp.bfloat16), w2_ref[...], preferred_element_type=f32)
    if final:
        ms2 = jnp.mean(acc * acc, axis=-1, keepdims=True)
        acc = acc * lax.rsqrt(ms2 + EPS) * gf_ref[...]
    o_ref[...] = acc


def _ffn_call(x, a, bm, woa, wob, g, w1, w2, gf, *, final):
    rows = x.shape[0]
    row_map = lambda i: (i, 0)
    const_map = lambda i: (0, 0)
    once = pl.Buffered(1)
    return pl.pallas_call(
        functools.partial(_ffn_kernel, final=final),
        grid=(rows // TM,),
        in_specs=[
            pl.BlockSpec((TM, D_MODEL), row_map),
            pl.BlockSpec((TM, D_A), row_map),
            pl.BlockSpec((TM, D_B), row_map),
            pl.BlockSpec((D_A, D_MODEL), const_map, pipeline_mode=once),
            pl.BlockSpec((D_B, D_MODEL), const_map, pipeline_mode=once),
            pl.BlockSpec((1, D_MODEL), const_map),
            pl.BlockSpec((D_MODEL, D_FF), const_map, pipeline_mode=once),
            pl.BlockSpec((D_FF, D_MODEL), const_map, pipeline_mode=once),
            pl.BlockSpec((1, D_MODEL), const_map),
        ],
        out_specs=pl.BlockSpec((TM, D_MODEL), row_map),
        out_shape=jax.ShapeDtypeStruct((rows, D_MODEL), jnp.float32),
        compiler_params=pltpu.CompilerParams(
            dimension_semantics=("arbitrary",), vmem_limit_bytes=VMEM_LIMIT),
        name="ffn",
    )(x, a, bm, woa, wob, g, w1, w2, gf)


def _rope_tables(pos):
    posf = pos.astype(jnp.float32)[:, None]
    lane = jnp.arange(LANES)

    def one(width, rot, limit):
        half = rot // 2
        d = lane % width
        inv = ROPE_THETA ** (-jnp.arange(half, dtype=jnp.float32) * (2.0 / rot))
        ang = posf * inv[None, :]
        cos_l = jnp.take(jnp.cos(ang), d % half, axis=1)
        sin_l = jnp.take(jnp.sin(ang), d % half, axis=1)
        live = lane < limit
        lo = jnp.logical_and(d < half, live)[None, :]
        hi = jnp.logical_and(jnp.logical_and(d >= half, d < rot), live)[None, :]
        c = jnp.where(jnp.logical_or(lo, hi), cos_l, 1.0)
        return [c, jnp.where(lo, -sin_l, 0.0), jnp.where(hi, sin_l, 0.0)]

    tabs = one(HEAD_DIM, ROT_DIM, LANES) + one(IDX_DIM, IDX_ROT, LANES) + one(IDX_DIM, IDX_ROT, IDX_DIM)
    return jnp.concatenate(tabs, axis=1)


def _dsa_layouts(q, qi, wi, nbatch, nblk):
    bf16 = jnp.bfloat16
    qT = q.reshape(nbatch, nblk, TQ, N_KV_HEADS, GQA_GROUP, HEAD_DIM).transpose(0, 1, 3, 5, 4, 2)
    qT = qT.reshape(nbatch, nblk, N_KV_HEADS, HEAD_DIM, GQA_GROUP * TQ).astype(bf16)
    qiT = qi.reshape(nbatch, nblk, TQ, N_IDX_HEADS, IDX_DIM).transpose(0, 1, 4, 3, 2)
    qiT = qiT.reshape(nbatch, nblk, IDX_DIM, N_IDX_HEADS * TQ).astype(bf16)
    wiT = wi.reshape(nbatch, nblk, TQ, N_IDX_HEADS).transpose(0, 1, 3, 2)
    wiT = wiT.reshape(nbatch, nblk, 1, N_IDX_HEADS * TQ)
    return qT, qiT, wiT


def _key_layouts(k, v, ki):
    bf16 = jnp.bfloat16
    nbatch, s = k.shape[:2]
    n_kt = s // KT
    kb = k.astype(bf16).reshape(nbatch, n_kt, KT, N_KV_HEADS, HEAD_DIM).transpose(0, 3, 1, 2, 4)
    vT = v.astype(bf16).reshape(nbatch, n_kt, KT, N_KV_HEADS, HEAD_DIM).transpose(0, 3, 1, 4, 2)
    kib = ki.astype(bf16).reshape(nbatch, n_kt, KT, IDX_DIM)
    return kb, vT, kib


def _from_oT(oT):
    nbatch, nblk = oT.shape[:2]
    o = oT.reshape(nbatch, nblk, N_KV_HEADS, HEAD_DIM, GQA_GROUP, TQ).transpose(0, 1, 5, 2, 4, 3)
    return o.reshape(nbatch, nblk, TQ, D_A)


def kernel(x_prompt, x_sample, cache_k, cache_v, cache_kidx, norm_mix, w_in, gate_norm, w_spatial,
           b_spatial, w_out, norm_ffn, w_ff1, w_ff2, norm_final):
    f32, bf16 = jnp.float32, jnp.bfloat16
    depth = w_in.shape[0]
    nb_p, t_p = x_prompt.shape[:2]
    nb_s, t_s = x_sample.shape[:2]
    past = cache_k.shape[2]
    rows_p, rows_s = nb_p * t_p, nb_s * t_s
    assert t_p % TM == 0 and rows_s % TM == 0 and TM % t_s == 0 and t_p % KT == 0
    assert t_s * 2 == TQ and t_s == CHUNK and past % CHUNK == 0
    nblk_p = t_p // TQ
    topk_p = min(TOPK_MAX, t_p // 4)
    topk_s = min(TOPK_MAX, (past + t_s) // 4)
    s_all = past + t_s
    s_pad = -(-s_all // KT) * KT

    n_head_cols = _U0 - 88
    w_in_p = jnp.concatenate(
        [w_in[:, :, :n_head_cols], jnp.zeros((depth, D_MODEL, _U0 - n_head_cols), w_in.dtype),
         w_in[:, :, n_head_cols:]], axis=2).astype(bf16)
    w_out_b = w_out.astype(bf16)
    w1_b = w_ff1.astype(bf16)
    w2_b = w_ff2.astype(bf16)
    cm = (jnp.arange(GMLP_CHUNK)[None, :] // CHUNK <= jnp.arange(GMLP_CHUNK)[:, None] // CHUNK)
    wm_p = jnp.where(cm[None, None], w_spatial, 0.0).astype(bf16)
    small = (jnp.arange(GMLP_CHUNK) < t_s)
    wm_s = jnp.where(jnp.logical_and(small[:, None], small[None, :])[None, None], w_spatial, 0.0).astype(bf16)
    wm_p = wm_p.reshape(depth, N_GROUPS_B // 2, 2 * GMLP_CHUNK, GMLP_CHUNK)
    wm_s = wm_s.reshape(depth, N_GROUPS_B // 2, 2 * GMLP_CHUNK, GMLP_CHUNK)
    bias_sp = jnp.repeat(jnp.swapaxes(b_spatial, 1, 2), GROUP_DIM_B, axis=2)

    pos_tab = jnp.concatenate([jnp.arange(t_p, dtype=jnp.int32),
                               jnp.tile(past + jnp.arange(t_s, dtype=jnp.int32), TM // t_s)])
    tab = _rope_tables(pos_tab)

    x = jnp.concatenate([x_prompt.reshape(rows_p, D_MODEL), x_sample.reshape(rows_s, D_MODEL)], axis=0)

    kp_l, vp_l, kip_l, ks_l, vs_l, kis_l, gvs_l = [], [], [], [], [], [], []
    for l in range(depth):
        q, k, v, kw, qi, gu, gvn = _proj_call(
            x, norm_mix[l][None, :], w_in_p[l], tab, gate_norm[l][None, :], rows_p // TM, t_p // TM)
        ki = kw[:, :IDX_DIM]
        wi = kw[:, IDX_DIM:IDX_DIM + N_IDX_HEADS] * (IDX_DIM ** -0.5 * N_IDX_HEADS ** -0.5)

        k_p = k[:rows_p].reshape(nb_p, t_p, N_KV_HEADS, HEAD_DIM)
        v_p = v[:rows_p].reshape(nb_p, t_p, N_KV_HEADS, HEAD_DIM)
        ki_p = ki[:rows_p].reshape(nb_p, t_p, IDX_DIM)
        qT, qiT, wiT = _dsa_layouts(q[:rows_p], qi[:rows_p], wi[:rows_p], nb_p, nblk_p)
        kb, vT, kib = _key_layouts(k_p, v_p, ki_p)
        oT_p, bmix_p = _dsa_call(
            qT, qiT, wiT, kb, vT, kib,
            gvn[:rows_p].reshape(nb_p, nblk_p, TQ, D_B), gu[:rows_p].reshape(nb_p, nblk_p, TQ, D_B),
            wm_p[l], bias_sp[l], causal=True, q_pos0=0, tq_real=TQ, topk=topk_p)
        a_p = _from_oT(oT_p).reshape(rows_p, D_A)

        k_s = k[rows_p:].reshape(nb_s, t_s, N_KV_HEADS, HEAD_DIM)
        v_s = v[rows_p:].reshape(nb_s, t_s, N_KV_HEADS, HEAD_DIM)
        ki_s = ki[rows_p:].reshape(nb_s, t_s, IDX_DIM)
        gvn_s = gvn[rows_p:].reshape(nb_s, t_s, D_B)

        def dup(a):
            a = a.reshape((nb_s, t_s) + a.shape[1:])
            return jnp.concatenate([a, a], axis=1).reshape((nb_s * TQ,) + a.shape[2:])

        qT, qiT, wiT = _dsa_layouts(dup(q[rows_p:]), dup(qi[rows_p:]), dup(wi[rows_p:]), nb_s, 1)
        padk = ((0, 0), (0, s_pad - s_all), (0, 0), (0, 0))
        k_all = jnp.pad(jnp.concatenate([cache_k[l], k_s], axis=1), padk)
        v_all = jnp.pad(jnp.concatenate([cache_v[l], v_s], axis=1), padk)
        ki_all = jnp.pad(jnp.concatenate([cache_kidx[l], ki_s], axis=1), padk[:3])
        kb, vT, kib = _key_layouts(k_all, v_all, ki_all)
        oT_s, bmix_s = _dsa_call(
            qT, qiT, wiT, kb, vT, kib,
            dup(gvn[rows_p:]).reshape(nb_s, 1, TQ, D_B), dup(gu[rows_p:]).reshape(nb_s, 1, TQ, D_B),
            wm_s[l], bias_sp[l], causal=False, q_pos0=past, tq_real=t_s, topk=topk_s)
        a_s = _from_oT(oT_s)[:, 0, :t_s].reshape(rows_s, D_A)
        bmix_s = bmix_s[:, 0, :t_s].reshape(rows_s, D_B)

        a_all = jnp.concatenate([a_p, a_s], axis=0).astype(bf16)
        bm_all = jnp.concatenate([bmix_p.reshape(rows_p, D_B), bmix_s], axis=0)
        x = _ffn_call(x, a_all, bm_all, w_out_b[l, :D_A], w_out_b[l, D_A:], norm_ffn[l][None, :],
                      w1_b[l], w2_b[l], norm_final[None, :], final=(l == depth - 1))

        kp_l.append(k_p)
        vp_l.append(v_p)
        kip_l.append(ki_p)
        ks_l.append(k_s)
        vs_l.append(v_s)
        kis_l.append(ki_s)
        gvs_l.append(gvn_s)

    y_prompt = x[:rows_p].reshape(nb_p, t_p, D_MODEL)
    y_sample = x[rows_p:].reshape(nb_s, t_s, D_MODEL)
    return (y_prompt, y_sample, jnp.stack(kp_l), jnp.stack(vp_l), jnp.stack(kip_l),
            jnp.stack(ks_l), jnp.stack(vs_l), jnp.stack(kis_l), jnp.stack(gvs_l))
```

```python
import functools

import numpy as np
import jax
import jax.numpy as jnp
from jax import lax
from jax.experimental import pallas as pl
from jax.experimental.pallas import tpu as pltpu

D_MODEL = 1024
CHUNK = 64
D_A = 512
D_B = 512
HEAD_DIM = 64
N_HEADS_A = 8
N_KV_HEADS = 2
GQA_GROUP = 4
D_KV = 128
ROT_DIM = 16
N_IDX_HEADS = 8
IDX_DIM = 32
IDX_ROT = 8
TOPK_MAX = 256
N_GROUPS_B = 8
GROUP_DIM_B = 64
GMLP_CHUNK = 128
D_FF = 4096
ROPE_THETA = 500000.0
EPS = 1e-6

LANES = 128
SUBLANES = 8
N_IN_PAD = 2176
TM = 512
TQ = 128
KT = 512
FAST_SEARCH_ITERS = 24
VMEM_LIMIT = 56 * 1024 * 1024

INT_MIN = np.int32(-2147483648)
MANT_MASK = np.int32(0x7FFFFFFF)
NEG_BIG = -1e30
LOG2E = 1.4426950408889634

_Q0, _K0, _V0, _QI0, _KW0, _U0, _GV0 = 0, 512, 640, 768, 1024, 1152, 1664


def _gelu(x):
    return x * (0.5 * (1.0 + jnp.tanh(0.7978845608028654 * (x + 0.044715 * (x * x * x)))))


def _rope128(x, c, s_lo, s_hi, half):
    return x * c + pltpu.roll(x, LANES - half, 1) * s_lo + pltpu.roll(x, half, 1) * s_hi


def _sortable(bits):
    return bits ^ ((bits >> 31) & MANT_MASK)


def _proj_kernel(x_ref, g_ref, w_ref, tab_ref, gn_ref,
                 qT_ref, qiT_ref, wi_ref, k_ref, v_ref, ki_ref, kb_ref, vT_ref, kib_ref, gu_ref, gvn_ref):
    bf16 = jnp.bfloat16
    x = x_ref[...]
    ms = jnp.mean(x * x, axis=-1, keepdims=True)
    xn = (x * lax.rsqrt(ms + EPS) * g_ref[...]).astype(bf16)
    z = jnp.dot(xn, w_ref[...], preferred_element_type=jnp.float32)

    def tab(j):
        return tab_ref[:, j * LANES:(j + 1) * LANES]

    cq, sq_lo, sq_hi = tab(0), tab(1), tab(2)
    ci, si_lo, si_hi = tab(3), tab(4), tab(5)
    ck, sk_lo, sk_hi = tab(6), tab(7), tab(8)
    n_blk = TM // TQ

    def blocks_T(slab):
        return [slab[b * TQ:(b + 1) * TQ, :].T for b in range(n_blk)]

    qT_ref[...] = jnp.zeros(qT_ref.shape, qT_ref.dtype)
    for j in range(D_A // LANES):
        zq = _rope128(z[:, _Q0 + j * LANES:_Q0 + (j + 1) * LANES], cq, sq_lo, sq_hi, ROT_DIM // 2)
        zq = zq * (HEAD_DIM ** -0.5 * LOG2E)
        for b, t in enumerate(blocks_T(zq)):
            for r in range(LANES // HEAD_DIM):
                h = j * (LANES // HEAD_DIM) + r
                g, hh = h // GQA_GROUP, h % GQA_GROUP
                qT_ref[b, g * HEAD_DIM:(g + 1) * HEAD_DIM, h * TQ:(h + 1) * TQ] = (
                    t[r * HEAD_DIM:(r + 1) * HEAD_DIM, :].astype(bf16))

    kk = _rope128(z[:, _K0:_K0 + LANES], cq, sq_lo, sq_hi, ROT_DIM // 2)
    k_ref[...] = kk
    kb_ref[...] = kk.astype(bf16)
    vv = z[:, _V0:_V0 + LANES]
    v_ref[...] = vv
    for b, t in enumerate(blocks_T(vv)):
        kt, half = (b * TQ) // KT, (b * TQ) % KT
        vT_ref[kt, :, half:half + TQ] = t.astype(bf16)

    for j in range(N_IDX_HEADS * IDX_DIM // LANES):
        zi = _rope128(z[:, _QI0 + j * LANES:_QI0 + (j + 1) * LANES], ci, si_lo, si_hi, IDX_ROT // 2)
        for b, t in enumerate(blocks_T(zi)):
            for r in range(LANES // IDX_DIM):
                h = j * (LANES // IDX_DIM) + r
                qiT_ref[b, :, h * TQ:(h + 1) * TQ] = t[r * IDX_DIM:(r + 1) * IDX_DIM, :].astype(bf16)

    kw = _rope128(z[:, _KW0:_KW0 + LANES], ck, sk_lo, sk_hi, IDX_ROT // 2)
    ki_ref[...] = kw[:, :IDX_DIM]
    kib_ref[...] = kw[:, :IDX_DIM].astype(bf16)
    for b, t in enumerate(blocks_T(kw)):
        wi_ref[b] = t[IDX_DIM:IDX_DIM + N_IDX_HEADS, :] * (IDX_DIM ** -0.5 * N_IDX_HEADS ** -0.5)

    gu_ref[...] = _gelu(z[:, _U0:_U0 + D_B]).astype(gu_ref.dtype)
    gg = _gelu(z[:, _GV0:_GV0 + D_B])
    gms = jnp.mean(gg * gg, axis=-1, keepdims=True)
    gvn_ref[...] = gg * lax.rsqrt(gms + EPS) * gn_ref[...]


def _proj_call(x, g, w, tab, gn, n_prompt_tiles, n_pos_tiles):
    rows = x.shape[0]
    n_tiles = rows // TM
    n_blk = TM // TQ
    f32, bf16 = jnp.float32, jnp.bfloat16

    def tab_map(i):
        return (jnp.where(i < n_prompt_tiles, i % n_pos_tiles, n_pos_tiles), 0)

    row2 = lambda i: (i, 0)
    const2 = lambda i: (0, 0)
    lead3 = lambda i: (i, 0, 0)
    outs = [
        ((rows // TQ, D_KV, N_HEADS_A * TQ), bf16, (n_blk, D_KV, N_HEADS_A * TQ), lead3),
        ((rows // TQ, IDX_DIM, N_IDX_HEADS * TQ), bf16, (n_blk, IDX_DIM, N_IDX_HEADS * TQ), lead3),
        ((rows // TQ, N_IDX_HEADS, TQ), f32, (n_blk, N_IDX_HEADS, TQ), lead3),
        ((rows, D_KV), f32, (TM, D_KV), row2),
        ((rows, D_KV), f32, (TM, D_KV), row2),
        ((rows, IDX_DIM), f32, (TM, IDX_DIM), row2),
        ((rows, D_KV), bf16, (TM, D_KV), row2),
        ((rows // KT, D_KV, KT), bf16, (TM // KT, D_KV, KT), lead3),
        ((rows, IDX_DIM), bf16, (TM, IDX_DIM), row2),
        ((rows, D_B), bf16, (TM, D_B), row2),
        ((rows, D_B), f32, (TM, D_B), row2),
    ]
    return pl.pallas_call(
        _proj_kernel,
        grid=(n_tiles,),
        in_specs=[
            pl.BlockSpec((TM, D_MODEL), row2),
            pl.BlockSpec((1, D_MODEL), const2),
            pl.BlockSpec((D_MODEL, N_IN_PAD), const2),
            pl.BlockSpec((TM, 9 * LANES), tab_map),
            pl.BlockSpec((1, D_B), const2),
        ],
        out_specs=tuple(pl.BlockSpec(blk, imap) for _, _, blk, imap in outs),
        out_shape=tuple(jax.ShapeDtypeStruct(shape, dt) for shape, dt, _, _ in outs),
        compiler_params=pltpu.CompilerParams(
            dimension_semantics=("arbitrary",), vmem_limit_bytes=VMEM_LIMIT),
        name="proj",
    )(x, g, w, tab, gn)


def _dsa_kernel(qT_ref, qiT_ref, wi_ref, k_ref, vT_ref, ki_ref, gvn_ref, gu_ref, wm_ref, bias_ref,
                ab_ref, skey_ref, sel_ref, *, causal, q_pos0, tq_real, n_keys, n_kt, topk):
    f32, bf16, i32 = jnp.float32, jnp.bfloat16, jnp.int32
    i = pl.program_id(1)
    if causal:
        n_tiles = ((i + 1) * TQ + KT - 1) // KT
        q0 = i * TQ
    else:
        n_tiles = n_kt
        q0 = q_pos0
    lane = lax.broadcasted_iota(i32, (1, TQ), 1)
    qchunk = (q0 + (lane & (tq_real - 1))) // CHUNK
    n_adm = jnp.minimum((qchunk + 1) * CHUNK, n_keys)

    def fold8(a, op):
        return op(a.reshape(a.shape[0] // SUBLANES, SUBLANES, a.shape[1]), axis=0)

    def score_tile(t, carry):
        kmax8, kmin8 = carry
        ki_t = ki_ref[t]
        r = jnp.maximum(jnp.dot(ki_t, qiT_ref[...], preferred_element_type=f32), 0.0)
        acc = r[:, :TQ] * wi_ref[0:1, :]
        for h in range(1, N_IDX_HEADS):
            acc = acc + r[:, h * TQ:(h + 1) * TQ] * wi_ref[h:h + 1, :]
        kpos = t * KT + lax.broadcasted_iota(i32, (KT, TQ), 0)
        adm = (kpos // CHUNK) <= qchunk
        key = _sortable(pltpu.bitcast(acc, i32))
        skey_ref[t] = jnp.where(adm, key, INT_MIN)
        kmax8 = jnp.maximum(kmax8, fold8(jnp.where(adm, key, INT_MIN), jnp.max))
        kmin8 = jnp.minimum(kmin8, fold8(jnp.where(adm, key, ~INT_MIN), jnp.min))
        return kmax8, kmin8

    kmax8, kmin8 = lax.fori_loop(
        0, n_tiles, score_tile,
        (jnp.full((SUBLANES, TQ), INT_MIN, i32), jnp.full((SUBLANES, TQ), ~INT_MIN, i32)))
    kmax = jnp.max(kmax8, axis=0, keepdims=True)
    kmin = jnp.min(kmin8, axis=0, keepdims=True)

    def count(pred_fn):
        def body(t, acc8):
            return acc8 + fold8(jnp.where(pred_fn(skey_ref[t]), 1, 0).astype(i32), jnp.sum)
        acc8 = lax.fori_loop(0, n_tiles, body, jnp.zeros((SUBLANES, TQ), i32))
        return jnp.sum(acc8, axis=0, keepdims=True)

    few = n_adm <= topk

    def n_open(done):
        return jnp.max(jnp.where(done == 0, 1.0, 0.0))

    def search_cond(st):
        return jnp.logical_and(st[-1] > 0, st[-2] < FAST_SEARCH_ITERS)

    def search_body(st):
        lo, hi, a, b, thr, cge, done, it, _ = st
        adj = jnp.logical_and(lo + 1 == hi, done == 0)
        thr = jnp.where(adj, lo, thr)
        cge = jnp.where(adj, a, cge)
        done = jnp.where(adj, 1, done)
        lo_v = pltpu.bitcast(_sortable(lo), f32)
        hi_v = pltpu.bitcast(_sortable(hi), f32)
        frac = jnp.where((it & 1) == 0, (a - topk).astype(f32) / (a - b).astype(f32), 0.5)
        c = _sortable(pltpu.bitcast(lo_v + (hi_v - lo_v) * frac, i32))
        c = jnp.minimum(jnp.maximum(c, lo + 1), hi - 1)
        c = jnp.where(done == 1, thr, c)
        cnt = count(lambda tile: tile >= c)
        live = done == 0
        found = jnp.logical_and(live, cnt == topk)
        up = jnp.logical_and(live, cnt > topk)
        dn = jnp.logical_and(live, cnt < topk)
        lo = jnp.where(up, c, lo)
        a = jnp.where(up, cnt, a)
        hi = jnp.where(dn, c, hi)
        b = jnp.where(dn, cnt, b)
        thr = jnp.where(found, c, thr)
        cge = jnp.where(found, cnt, cge)
        done = jnp.where(found, 1, done)
        return lo, hi, a, b, thr, cge, done, it + 1, n_open(done)

    zero = jnp.zeros((1, TQ), i32)
    done0 = jnp.where(few, 1, 0)
    st = lax.while_loop(search_cond, search_body,
                        (kmin, kmax + 1, n_adm, zero, zero + (INT_MIN + 1), zero, done0,
                         jnp.int32(0), n_open(done0)))
    lo, hi, a, _, thr, cge, done = st[:7]
    adj = jnp.logical_and(lo + 1 == hi, done == 0)
    thr = jnp.where(adj, lo, thr)
    cge = jnp.where(adj, a, cge)
    done = jnp.where(adj, 1, done)
    sel_ref[0:1, :] = thr
    sel_ref[1:2, :] = cge

    @pl.when(n_open(done) > 0)
    def _():
        def bit_body(j, bst):
            t_u, cnt_at = bst
            c_u = t_u | jnp.left_shift(jnp.int32(1), 31 - j)
            cnt = count(lambda tile: tile >= (c_u ^ INT_MIN))
            ok = cnt >= topk
            return jnp.where(ok, c_u, t_u), jnp.where(ok, cnt, cnt_at)

        t_u, cnt_at = lax.fori_loop(0, 32, bit_body, (zero, zero))
        sel_ref[0:1, :] = jnp.where(done == 1, thr, jnp.maximum(t_u ^ INT_MIN, INT_MIN + 1))
        sel_ref[1:2, :] = jnp.where(done == 1, cge, cnt_at)

    thr = sel_ref[0:1, :]
    cge = sel_ref[1:2, :]

    @pl.when(jnp.max(jnp.where(cge > topk, 1.0, 0.0)) > 0)
    def _():
        need = (topk - count(lambda tile: tile > thr)).astype(f32)
        rr = lax.broadcasted_iota(i32, (KT, KT), 0)
        cc = lax.broadcasted_iota(i32, (KT, KT), 1)
        tri = jnp.where(cc <= rr, 1.0, 0.0).astype(bf16)

        def body(t, seen):
            tile = skey_ref[t]
            eq = tile == thr
            eqf = jnp.where(eq, 1.0, 0.0).astype(bf16)
            rank = jnp.dot(tri, eqf, preferred_element_type=f32) + seen
            skey_ref[t] = jnp.where(jnp.logical_and(eq, rank > need), INT_MIN, tile)
            return rank[KT - 1:KT, :]

        lax.fori_loop(0, n_tiles, body, jnp.zeros((1, TQ), f32))

    n_hq = N_HEADS_A * TQ
    g_hq = GQA_GROUP * TQ

    def logits(t):
        bias = jnp.where(skey_ref[t] >= thr, 0.0, -jnp.inf)
        bias = jnp.concatenate([bias] * N_HEADS_A, axis=1)
        return jnp.dot(k_ref[t], qT_ref[...], preferred_element_type=f32) + bias

    m8 = lax.fori_loop(0, n_tiles, lambda t, m: jnp.maximum(m, fold8(logits(t), jnp.max)),
                       jnp.full((SUBLANES, n_hq), -jnp.inf, f32))
    mrow = jnp.maximum(jnp.max(m8, axis=0, keepdims=True), NEG_BIG)

    def pv_tile(t, st):
        l8, acc0, acc1 = st
        pr = jnp.exp2(logits(t) - mrow)
        l8 = l8 + fold8(pr, jnp.sum)
        prb = pr.astype(bf16)
        acc0 = acc0 + jnp.dot(vT_ref[t, :HEAD_DIM, :], prb[:, :g_hq], preferred_element_type=f32)
        acc1 = acc1 + jnp.dot(vT_ref[t, HEAD_DIM:, :], prb[:, g_hq:], preferred_element_type=f32)
        return l8, acc0, acc1

    l8, acc0, acc1 = lax.fori_loop(
        0, n_tiles, pv_tile,
        (jnp.zeros((SUBLANES, n_hq), f32), jnp.zeros((HEAD_DIM, g_hq), f32), jnp.zeros((HEAD_DIM, g_hq), f32)))
    linv = 1.0 / jnp.sum(l8, axis=0, keepdims=True)
    oT = [acc0 * linv[:, :g_hq], acc1 * linv[:, g_hq:]]
    for j in range(D_A // LANES):
        g, hh = (2 * j) // GQA_GROUP, (2 * j) % GQA_GROUP
        pair = oT[g][:, hh * TQ:(hh + 2) * TQ]
        stacked = jnp.concatenate([pair[:, :TQ], pair[:, TQ:]], axis=0)
        ab_ref[:, j * LANES:(j + 1) * LANES] = stacked.T.astype(ab_ref.dtype)

    lane_c = lax.broadcasted_iota(i32, (GMLP_CHUNK, LANES), 1)
    for p2 in range(N_GROUPS_B // 2):
        xg = gvn_ref[:, p2 * LANES:(p2 + 1) * LANES].astype(bf16)
        both = jnp.dot(wm_ref[p2], xg, preferred_element_type=f32)
        mix = jnp.where(lane_c < GROUP_DIM_B, both[:GMLP_CHUNK], both[GMLP_CHUNK:])
        mix = mix + bias_ref[:, p2 * LANES:(p2 + 1) * LANES]
        gu = gu_ref[:, p2 * LANES:(p2 + 1) * LANES].astype(f32)
        ab_ref[:, D_A + p2 * LANES:D_A + (p2 + 1) * LANES] = (gu * mix).astype(ab_ref.dtype)


def _dsa_call(qT, qiT, wi, k, vT, ki, gvn, gu, wm, bias, *, causal, q_pos0, tq_real, n_keys, topk):
    nbatch, nblk = qT.shape[:2]
    n_kt = k.shape[1]
    kern = functools.partial(_dsa_kernel, causal=causal, q_pos0=q_pos0, tq_real=tq_real,
                             n_keys=n_keys, n_kt=n_kt, topk=topk)
    blk4 = lambda b, i: (b, i, 0, 0)
    per_b = lambda b, i: (b, 0, 0, 0)
    return pl.pallas_call(
        kern,
        grid=(nbatch, nblk),
        in_specs=[
            pl.BlockSpec((None, None, D_KV, N_HEADS_A * TQ), blk4),
            pl.BlockSpec((None, None, IDX_DIM, N_IDX_HEADS * TQ), blk4),
            pl.BlockSpec((None, None, N_IDX_HEADS, TQ), blk4),
            pl.BlockSpec((None, n_kt, KT, D_KV), per_b),
            pl.BlockSpec((None, n_kt, D_KV, KT), per_b),
            pl.BlockSpec((None, n_kt, KT, IDX_DIM), per_b),
            pl.BlockSpec((None, None, TQ, D_B), blk4),
            pl.BlockSpec((None, None, TQ, D_B), blk4),
            pl.BlockSpec((N_GROUPS_B // 2, 2 * GMLP_CHUNK, GMLP_CHUNK), lambda b, i: (0, 0, 0)),
            pl.BlockSpec((GMLP_CHUNK, D_B), lambda b, i: (0, 0)),
        ],
        out_specs=pl.BlockSpec((None, None, TQ, D_A + D_B), blk4),
        out_shape=jax.ShapeDtypeStruct((nbatch, nblk, TQ, D_A + D_B), jnp.bfloat16),
        scratch_shapes=[pltpu.VMEM((n_kt, KT, TQ), jnp.int32),
                        pltpu.VMEM((SUBLANES, TQ), jnp.int32)],
        compiler_params=pltpu.CompilerParams(
            dimension_semantics=("arbitrary", "arbitrary"), vmem_limit_bytes=VMEM_LIMIT),
        name="dsa_prompt" if causal else "dsa_sample",
    )(qT, qiT, wi, k, vT, ki, gvn, gu, wm, bias)


def _ffn_kernel(x_ref, ab_ref, wo_ref, g_ref, w1_ref, w2_ref, gf_ref, o_ref, *, final):
    f32 = jnp.float32
    x = x_ref[...] + jnp.dot(ab_ref[...], wo_ref[...], preferred_element_type=f32)
    ms = jnp.mean(x * x, axis=-1, keepdims=True)
    xn = (x * lax.rsqrt(ms + EPS) * g_ref[...]).astype(jnp.bfloat16)
    h = jnp.maximum(jnp.dot(xn, w1_ref[...], preferred_element_type=f32), 0.0)
    acc = x + jnp.dot((h * h).astype(jnp.bfloat16), w2_ref[...], preferred_element_type=f32)
    if final:
        ms2 = jnp.mean(acc * acc, axis=-1, keepdims=True)
        acc = acc * lax.rsqrt(ms2 + EPS) * gf_ref[...]
    o_ref[...] = acc


def _ffn_call(x, ab, wo, g, w1, w2, gf, *, final):
    rows = x.shape[0]
    row_map = lambda i: (i, 0)
    const_map = lambda i: (0, 0)
    once = pl.Buffered(1)
    return pl.pallas_call(
        functools.partial(_ffn_kernel, final=final),
        grid=(rows // TM,),
        in_specs=[
            pl.BlockSpec((TM, D_MODEL), row_map),
            pl.BlockSpec((TM, D_A + D_B), row_map),
            pl.BlockSpec((D_A + D_B, D_MODEL), const_map, pipeline_mode=once),
            pl.BlockSpec((1, D_MODEL), const_map),
            pl.BlockSpec((D_MODEL, D_FF), const_map, pipeline_mode=once),
            pl.BlockSpec((D_FF, D_MODEL), const_map, pipeline_mode=once),
            pl.BlockSpec((1, D_MODEL), const_map),
        ],
        out_specs=pl.BlockSpec((TM, D_MODEL), row_map),
        out_shape=jax.ShapeDtypeStruct((rows, D_MODEL), jnp.float32),
        compiler_params=pltpu.CompilerParams(
            dimension_semantics=("arbitrary",), vmem_limit_bytes=VMEM_LIMIT),
        name="ffn",
    )(x, ab, wo, g, w1, w2, gf)


def _rope_tables(pos):
    posf = pos.astype(jnp.float32)[:, None]
    lane = jnp.arange(LANES)

    def one(width, rot, limit):
        half = rot // 2
        d = lane % width
        inv = ROPE_THETA ** (-jnp.arange(half, dtype=jnp.float32) * (2.0 / rot))
        ang = posf * inv[None, :]
        cos_l = jnp.take(jnp.cos(ang), d % half, axis=1)
        sin_l = jnp.take(jnp.sin(ang), d % half, axis=1)
        live = lane < limit
        lo = jnp.logical_and(d < half, live)[None, :]
        hi = jnp.logical_and(jnp.logical_and(d >= half, d < rot), live)[None, :]
        c = jnp.where(jnp.logical_or(lo, hi), cos_l, 1.0)
        return [c, jnp.where(lo, -sin_l, 0.0), jnp.where(hi, sin_l, 0.0)]

    tabs = one(HEAD_DIM, ROT_DIM, LANES) + one(IDX_DIM, IDX_ROT, LANES) + one(IDX_DIM, IDX_ROT, IDX_DIM)
    return jnp.concatenate(tabs, axis=1)


def kernel(x_prompt, x_sample, cache_k, cache_v, cache_kidx, norm_mix, w_in, gate_norm, w_spatial,
           b_spatial, w_out, norm_ffn, w_ff1, w_ff2, norm_final):
    f32, bf16 = jnp.float32, jnp.bfloat16
    depth = w_in.shape[0]
    nb_p, t_p = x_prompt.shape[:2]
    nb_s, t_s = x_sample.shape[:2]
    past = cache_k.shape[2]
    rows_p, rows_s = nb_p * t_p, nb_s * TQ
    assert t_p % TM == 0 and rows_s % TM == 0 and t_p % KT == 0
    assert t_s * 2 == TQ and t_s == CHUNK and past % CHUNK == 0
    nblk_p = t_p // TQ
    topk_p = min(TOPK_MAX, t_p // 4)
    topk_s = min(TOPK_MAX, (past + t_s) // 4)
    s_all = past + t_s
    s_pad = -(-s_all // KT) * KT

    n_head_cols = _U0 - 88
    w_in_p = jnp.concatenate(
        [w_in[:, :, :n_head_cols], jnp.zeros((depth, D_MODEL, _U0 - n_head_cols), w_in.dtype),
         w_in[:, :, n_head_cols:]], axis=2).astype(bf16)
    w_out_b = w_out.astype(bf16)
    w1_b = w_ff1.astype(bf16)
    w2_b = w_ff2.astype(bf16)
    cm = (jnp.arange(GMLP_CHUNK)[None, :] // CHUNK <= jnp.arange(GMLP_CHUNK)[:, None] // CHUNK)
    wm_p = jnp.where(cm[None, None], w_spatial, 0.0).astype(bf16)
    small = (jnp.arange(GMLP_CHUNK) < t_s)
    wm_s = jnp.where(jnp.logical_and(small[:, None], small[None, :])[None, None], w_spatial, 0.0).astype(bf16)
    wm_p = wm_p.reshape(depth, N_GROUPS_B // 2, 2 * GMLP_CHUNK, GMLP_CHUNK)
    wm_s = wm_s.reshape(depth, N_GROUPS_B // 2, 2 * GMLP_CHUNK, GMLP_CHUNK)
    bias_sp = jnp.repeat(jnp.swapaxes(b_spatial, 1, 2), GROUP_DIM_B, axis=2)

    pos_s = past + jnp.arange(t_s, dtype=jnp.int32)
    pos_tab = jnp.concatenate([jnp.arange(t_p, dtype=jnp.int32),
                               jnp.tile(jnp.concatenate([pos_s, pos_s]), TM // TQ)])
    tab = _rope_tables(pos_tab)

    cache_kb = cache_k.astype(bf16).reshape(depth, nb_s, past, D_KV)
    cache_vb = cache_v.astype(bf16).reshape(depth, nb_s, past, D_KV)
    cache_kib = cache_kidx.astype(bf16)
    pad_rows = jnp.zeros((nb_s, s_pad - s_all, D_KV), bf16)

    x = jnp.concatenate([x_prompt.reshape(rows_p, D_MODEL),
                         jnp.concatenate([x_sample, x_sample], axis=1).reshape(rows_s, D_MODEL)], axis=0)

    def first_half(a):
        return a.reshape((nb_s, TQ) + a.shape[1:])[:, :t_s]

    kp_l, vp_l, kip_l, ks_l, vs_l, kis_l, gvs_l = [], [], [], [], [], [], []
    for l in range(depth):
        qT, qiT, wiT, k, v, ki, kb, vT, kib, gu, gvn = _proj_call(
            x, norm_mix[l][None, :], w_in_p[l], tab, gate_norm[l][None, :], rows_p // TM, t_p // TM)
        nq_p = rows_p // TQ

        ab_p = _dsa_call(
            qT[:nq_p].reshape((nb_p, nblk_p) + qT.shape[1:]),
            qiT[:nq_p].reshape((nb_p, nblk_p) + qiT.shape[1:]),
            wiT[:nq_p].reshape((nb_p, nblk_p) + wiT.shape[1:]),
            kb[:rows_p].reshape(nb_p, t_p // KT, KT, D_KV),
            vT[:rows_p // KT].reshape(nb_p, t_p // KT, D_KV, KT),
            kib[:rows_p].reshape(nb_p, t_p // KT, KT, IDX_DIM),
            gvn[:rows_p].reshape(nb_p, nblk_p, TQ, D_B), gu[:rows_p].reshape(nb_p, nblk_p, TQ, D_B),
            wm_p[l], bias_sp[l], causal=True, q_pos0=0, tq_real=TQ, n_keys=t_p, topk=topk_p)

        k_s = first_half(k[rows_p:])
        v_s = first_half(v[rows_p:])
        ki_s = first_half(ki[rows_p:])
        k_all = jnp.concatenate([cache_kb[l], first_half(kb[rows_p:]), pad_rows], axis=1)
        v_all = jnp.concatenate([cache_vb[l], v_s.astype(bf16), pad_rows], axis=1)
        ki_all = jnp.concatenate([cache_kib[l], first_half(kib[rows_p:]), pad_rows[:, :, :IDX_DIM]], axis=1)
        ab_s = _dsa_call(
            qT[nq_p:].reshape((nb_s, 1) + qT.shape[1:]),
            qiT[nq_p:].reshape((nb_s, 1) + qiT.shape[1:]),
            wiT[nq_p:].reshape((nb_s, 1) + wiT.shape[1:]),
            k_all.reshape(nb_s, s_pad // KT, KT, D_KV),
            jnp.swapaxes(v_all.reshape(nb_s, s_pad // KT, KT, D_KV), 2, 3),
            ki_all.reshape(nb_s, s_pad // KT, KT, IDX_DIM),
            gvn[rows_p:].reshape(nb_s, 1, TQ, D_B), gu[rows_p:].reshape(nb_s, 1, TQ, D_B),
            wm_s[l], bias_sp[l], causal=False, q_pos0=past, tq_real=t_s, n_keys=s_all, topk=topk_s)

        ab = jnp.concatenate([ab_p.reshape(rows_p, D_A + D_B), ab_s.reshape(rows_s, D_A + D_B)], axis=0)
        x = _ffn_call(x, ab, w_out_b[l], norm_ffn[l][None, :], w1_b[l], w2_b[l], norm_final[None, :],
                      final=(l == depth - 1))

        kp_l.append(k[:rows_p].reshape(nb_p, t_p, N_KV_HEADS, HEAD_DIM))
        vp_l.append(v[:rows_p].reshape(nb_p, t_p, N_KV_HEADS, HEAD_DIM))
        kip_l.append(ki[:rows_p].reshape(nb_p, t_p, IDX_DIM))
        ks_l.append(k_s.reshape(nb_s, t_s, N_KV_HEADS, HEAD_DIM))
        vs_l.append(v_s.reshape(nb_s, t_s, N_KV_HEADS, HEAD_DIM))
        kis_l.append(ki_s)
        gvs_l.append(first_half(gvn[rows_p:]))

    y_prompt = x[:rows_p].reshape(nb_p, t_p, D_MODEL)
    y_sample = first_half(x[rows_p:])
    return (y_prompt, y_sample, jnp.stack(kp_l), jnp.stack(vp_l), jnp.stack(kip_l),
            jnp.stack(ks_l), jnp.stack(vs_l), jnp.stack(kis_l), jnp.stack(gvs_l))
```

```python
import functools

import numpy as np
import jax
import jax.numpy as jnp
from jax import lax
from jax.experimental import pallas as pl
from jax.experimental.pallas import tpu as pltpu

D_MODEL = 1024
CHUNK = 64
D_A = 512
D_B = 512
HEAD_DIM = 64
N_HEADS_A = 8
N_KV_HEADS = 2
GQA_GROUP = 4
D_KV = 128
ROT_DIM = 16
N_IDX_HEADS = 8
IDX_DIM = 32
IDX_ROT = 8
TOPK_MAX = 256
N_GROUPS_B = 8
GROUP_DIM_B = 64
GMLP_CHUNK = 128
D_FF = 4096
ROPE_THETA = 500000.0
EPS = 1e-6

LANES = 128
SUBLANES = 8
N_IN_PAD = 2176
TM = 512
TQ = 128
KT = 512
FAST_SEARCH_ITERS = 40
VMEM_LIMIT = 56 * 1024 * 1024

INT_MIN = np.int32(-2147483648)
MANT_MASK = np.int32(0x7FFFFFFF)
NEG_BIG = -1e30
LOG2E = 1.4426950408889634

_Q0, _K0, _V0, _QI0, _KW0, _U0, _GV0 = 0, 512, 640, 768, 1024, 1152, 1664


def _gelu(x):
    return x * (0.5 * (1.0 + jnp.tanh(0.7978845608028654 * (x + 0.044715 * (x * x * x)))))


def _rope128(x, c, s_lo, s_hi, half):
    return x * c + pltpu.roll(x, LANES - half, 1) * s_lo + pltpu.roll(x, half, 1) * s_hi


def _sortable(bits):
    return bits ^ ((bits >> 31) & MANT_MASK)


def _proj_kernel(x_ref, g_ref, w_ref, tab_ref, gn_ref,
                 qT_ref, qiT_ref, wi_ref, k_ref, v_ref, ki_ref, kb_ref, vT_ref, kib_ref, gu_ref, gvn_ref):
    bf16 = jnp.bfloat16
    x = x_ref[...]
    ms = jnp.mean(x * x, axis=-1, keepdims=True)
    xn = (x * lax.rsqrt(ms + EPS) * g_ref[...]).astype(bf16)
    z = jnp.dot(xn, w_ref[...], preferred_element_type=jnp.float32)

    def tab(j):
        return tab_ref[:, j * LANES:(j + 1) * LANES]

    cq, sq_lo, sq_hi = tab(0), tab(1), tab(2)
    ci, si_lo, si_hi = tab(3), tab(4), tab(5)
    ck, sk_lo, sk_hi = tab(6), tab(7), tab(8)
    n_blk = TM // TQ

    def blocks_T(slab):
        return [slab[b * TQ:(b + 1) * TQ, :].T for b in range(n_blk)]

    qT_ref[...] = jnp.zeros(qT_ref.shape, qT_ref.dtype)
    for j in range(D_A // LANES):
        zq = _rope128(z[:, _Q0 + j * LANES:_Q0 + (j + 1) * LANES], cq, sq_lo, sq_hi, ROT_DIM // 2)
        zq = zq * (HEAD_DIM ** -0.5 * LOG2E)
        for b, t in enumerate(blocks_T(zq)):
            for r in range(LANES // HEAD_DIM):
                h = j * (LANES // HEAD_DIM) + r
                g, hh = h // GQA_GROUP, h % GQA_GROUP
                qT_ref[b, g * HEAD_DIM:(g + 1) * HEAD_DIM, h * TQ:(h + 1) * TQ] = (
                    t[r * HEAD_DIM:(r + 1) * HEAD_DIM, :].astype(bf16))

    kk = _rope128(z[:, _K0:_K0 + LANES], cq, sq_lo, sq_hi, ROT_DIM // 2)
    k_ref[...] = kk
    kb_ref[...] = kk.astype(bf16)
    vv = z[:, _V0:_V0 + LANES]
    v_ref[...] = vv
    for b, t in enumerate(blocks_T(vv)):
        kt, half = (b * TQ) // KT, (b * TQ) % KT
        vT_ref[kt, :, half:half + TQ] = t.astype(bf16)

    for j in range(N_IDX_HEADS * IDX_DIM // LANES):
        zi = _rope128(z[:, _QI0 + j * LANES:_QI0 + (j + 1) * LANES], ci, si_lo, si_hi, IDX_ROT // 2)
        for b, t in enumerate(blocks_T(zi)):
            for r in range(LANES // IDX_DIM):
                h = j * (LANES // IDX_DIM) + r
                qiT_ref[b, :, h * TQ:(h + 1) * TQ] = t[r * IDX_DIM:(r + 1) * IDX_DIM, :].astype(bf16)

    kw = _rope128(z[:, _KW0:_KW0 + LANES], ck, sk_lo, sk_hi, IDX_ROT // 2)
    ki_ref[...] = kw[:, :IDX_DIM]
    kib_ref[...] = kw[:, :IDX_DIM].astype(bf16)
    for b, t in enumerate(blocks_T(kw)):
        wi_ref[b] = t[IDX_DIM:IDX_DIM + N_IDX_HEADS, :] * (IDX_DIM ** -0.5 * N_IDX_HEADS ** -0.5)

    gu_ref[...] = _gelu(z[:, _U0:_U0 + D_B]).astype(gu_ref.dtype)
    gg = _gelu(z[:, _GV0:_GV0 + D_B])
    gms = jnp.mean(gg * gg, axis=-1, keepdims=True)
    gvn_ref[...] = gg * lax.rsqrt(gms + EPS) * gn_ref[...]


def _proj_call(x, g, w, tab, gn, n_prompt_tiles, n_pos_tiles):
    rows = x.shape[0]
    n_tiles = rows // TM
    n_blk = TM // TQ
    f32, bf16 = jnp.float32, jnp.bfloat16

    def tab_map(i):
        return (jnp.where(i < n_prompt_tiles, i % n_pos_tiles, n_pos_tiles), 0)

    row2 = lambda i: (i, 0)
    const2 = lambda i: (0, 0)
    lead3 = lambda i: (i, 0, 0)
    outs = [
        ((rows // TQ, D_KV, N_HEADS_A * TQ), bf16, (n_blk, D_KV, N_HEADS_A * TQ), lead3),
        ((rows // TQ, IDX_DIM, N_IDX_HEADS * TQ), bf16, (n_blk, IDX_DIM, N_IDX_HEADS * TQ), lead3),
        ((rows // TQ, N_IDX_HEADS, TQ), f32, (n_blk, N_IDX_HEADS, TQ), lead3),
        ((rows, D_KV), f32, (TM, D_KV), row2),
        ((rows, D_KV), f32, (TM, D_KV), row2),
        ((rows, IDX_DIM), f32, (TM, IDX_DIM), row2),
        ((rows, D_KV), bf16, (TM, D_KV), row2),
        ((rows // KT, D_KV, KT), bf16, (TM // KT, D_KV, KT), lead3),
        ((rows, IDX_DIM), bf16, (TM, IDX_DIM), row2),
        ((rows, D_B), bf16, (TM, D_B), row2),
        ((rows, D_B), f32, (TM, D_B), row2),
    ]
    return pl.pallas_call(
        _proj_kernel,
        grid=(n_tiles,),
        in_specs=[
            pl.BlockSpec((TM, D_MODEL), row2),
            pl.BlockSpec((1, D_MODEL), const2),
            pl.BlockSpec((D_MODEL, N_IN_PAD), const2),
            pl.BlockSpec((TM, 9 * LANES), tab_map),
            pl.BlockSpec((1, D_B), const2),
        ],
        out_specs=tuple(pl.BlockSpec(blk, imap) for _, _, blk, imap in outs),
        out_shape=tuple(jax.ShapeDtypeStruct(shape, dt) for shape, dt, _, _ in outs),
        compiler_params=pltpu.CompilerParams(
            dimension_semantics=("arbitrary",), vmem_limit_bytes=VMEM_LIMIT),
        name="proj",
    )(x, g, w, tab, gn)


def _dsa_kernel(qT_ref, qiT_ref, wi_ref, k_ref, vT_ref, ki_ref, gvn_ref, gu_ref, wm_ref, bias_ref,
                ab_ref, skey_ref, sel_ref, *, causal, q_pos0, tq_real, n_keys, n_kt, topk):
    f32, bf16, i32 = jnp.float32, jnp.bfloat16, jnp.int32
    i = pl.program_id(1)
    if causal:
        n_tiles = ((i + 1) * TQ + KT - 1) // KT
        q0 = i * TQ
    else:
        n_tiles = n_kt
        q0 = q_pos0
    lane = lax.broadcasted_iota(i32, (1, TQ), 1)
    qchunk = (q0 + (lane & (tq_real - 1))) // CHUNK
    n_adm = jnp.minimum((qchunk + 1) * CHUNK, n_keys)

    def fold8(a, op):
        return op(a.reshape(a.shape[0] // SUBLANES, SUBLANES, a.shape[1]), axis=0)

    def score_tile(t, carry):
        kmax8, kmin8, nz8, np8 = carry
        ki_t = ki_ref[t]
        r = jnp.maximum(jnp.dot(ki_t, qiT_ref[...], preferred_element_type=f32), 0.0)
        acc = r[:, :TQ] * wi_ref[0:1, :]
        for h in range(1, N_IDX_HEADS):
            acc = acc + r[:, h * TQ:(h + 1) * TQ] * wi_ref[h:h + 1, :]
        kpos = t * KT + lax.broadcasted_iota(i32, (KT, TQ), 0)
        adm = (kpos // CHUNK) <= qchunk
        acc = jnp.where(acc == 0.0, 0.0, acc)
        key = _sortable(pltpu.bitcast(acc, i32))
        mkey = jnp.where(adm, key, INT_MIN)
        skey_ref[t] = mkey
        kmax8 = jnp.maximum(kmax8, fold8(mkey, jnp.max))
        kmin8 = jnp.minimum(kmin8, fold8(jnp.where(adm, key, ~INT_MIN), jnp.min))
        nz8 = nz8 + fold8(jnp.where(mkey >= 0, 1, 0).astype(i32), jnp.sum)
        np8 = np8 + fold8(jnp.where(mkey >= 1, 1, 0).astype(i32), jnp.sum)
        return kmax8, kmin8, nz8, np8

    zero8 = jnp.zeros((SUBLANES, TQ), i32)
    kmax8, kmin8, nz8, np8 = lax.fori_loop(
        0, n_tiles, score_tile,
        (jnp.full((SUBLANES, TQ), INT_MIN, i32), jnp.full((SUBLANES, TQ), ~INT_MIN, i32), zero8, zero8))
    kmax = jnp.max(kmax8, axis=0, keepdims=True)
    kmin = jnp.min(kmin8, axis=0, keepdims=True)
    n_ge0 = jnp.sum(nz8, axis=0, keepdims=True)
    n_gt0 = jnp.sum(np8, axis=0, keepdims=True)

    def count(pred_fn):
        def body(t, acc8):
            return acc8 + fold8(jnp.where(pred_fn(skey_ref[t]), 1, 0).astype(i32), jnp.sum)
        acc8 = lax.fori_loop(0, n_tiles, body, jnp.zeros((SUBLANES, TQ), i32))
        return jnp.sum(acc8, axis=0, keepdims=True)

    few = n_adm <= topk
    tie0 = jnp.logical_and(n_gt0 < topk, n_ge0 >= topk)
    hit1 = n_gt0 == topk
    above = n_gt0 > topk

    def n_open(done):
        return jnp.max(jnp.where(done == 0, 1.0, 0.0))

    def search_cond(st):
        return jnp.logical_and(st[-1] > 0, st[-2] < FAST_SEARCH_ITERS)

    def search_body(st):
        lo, hi, a, b, thr, cge, done, it, _ = st
        adj = jnp.logical_and(lo + 1 == hi, done == 0)
        thr = jnp.where(adj, lo, thr)
        cge = jnp.where(adj, a, cge)
        done = jnp.where(adj, 1, done)
        lo_v = pltpu.bitcast(_sortable(lo), f32)
        hi_v = pltpu.bitcast(_sortable(hi), f32)
        af, bf = a.astype(f32), b.astype(f32)
        lin = (af - topk) / (af - bf)
        la = jnp.log(af + 0.5)
        lg = (la - np.float32(np.log(topk))) / (la - jnp.log(bf + 0.5))
        frac = jnp.where(af > 4.0 * jnp.maximum(bf, 1.0), lg, lin)
        c_int = _sortable(pltpu.bitcast(lo_v + (hi_v - lo_v) * frac, i32))
        c_mid = (lo >> 1) + (hi >> 1) + (lo & hi & 1)
        c = jnp.where((it & 1) == 0, c_int, c_mid)
        c = jnp.minimum(jnp.maximum(c, lo + 1), hi - 1)
        c = jnp.where(done == 1, thr, c)
        cnt = count(lambda tile: tile >= c)
        live = done == 0
        found = jnp.logical_and(live, cnt == topk)
        up = jnp.logical_and(live, cnt > topk)
        dn = jnp.logical_and(live, cnt < topk)
        lo = jnp.where(up, c, lo)
        a = jnp.where(up, cnt, a)
        hi = jnp.where(dn, c, hi)
        b = jnp.where(dn, cnt, b)
        thr = jnp.where(found, c, thr)
        cge = jnp.where(found, cnt, cge)
        done = jnp.where(found, 1, done)
        return lo, hi, a, b, thr, cge, done, it + 1, n_open(done)

    zero = jnp.zeros((1, TQ), i32)
    done0 = jnp.where(jnp.logical_or(few, jnp.logical_or(tie0, hit1)), 1, 0)
    thr0 = jnp.where(few, INT_MIN + 1, jnp.where(tie0, 0, 1))
    cge0 = jnp.where(few, 0, jnp.where(tie0, n_ge0, topk))
    st = lax.while_loop(search_cond, search_body,
                        (jnp.where(above, 1, kmin), jnp.where(above, kmax + 1, 0),
                         jnp.where(above, n_gt0, n_adm), jnp.where(above, 0, n_ge0),
                         thr0, cge0, done0, jnp.int32(0), n_open(done0)))
    lo, hi, a, _, thr, cge, done = st[:7]
    adj = jnp.logical_and(lo + 1 == hi, done == 0)
    thr = jnp.where(adj, lo, thr)
    cge = jnp.where(adj, a, cge)
    done = jnp.where(adj, 1, done)
    sel_ref[0:1, :] = thr
    sel_ref[1:2, :] = cge

    @pl.when(n_open(done) > 0)
    def _():
        def bit_body(j, bst):
            t_u, cnt_at = bst
            c_u = t_u | jnp.left_shift(jnp.int32(1), 31 - j)
            cnt = count(lambda tile: tile >= (c_u ^ INT_MIN))
            ok = cnt >= topk
            return jnp.where(ok, c_u, t_u), jnp.where(ok, cnt, cnt_at)

        t_u, cnt_at = lax.fori_loop(0, 32, bit_body, (zero, zero))
        sel_ref[0:1, :] = jnp.where(done == 1, thr, jnp.maximum(t_u ^ INT_MIN, INT_MIN + 1))
        sel_ref[1:2, :] = jnp.where(done == 1, cge, cnt_at)

    thr = sel_ref[0:1, :]
    cge = sel_ref[1:2, :]

    @pl.when(jnp.max(jnp.where(cge > topk, 1.0, 0.0)) > 0)
    def _():
        need = (topk - count(lambda tile: tile > thr)).astype(f32)
        rr = lax.broadcasted_iota(i32, (KT, KT), 0)
        cc = lax.broadcasted_iota(i32, (KT, KT), 1)
        tri = jnp.where(cc <= rr, 1.0, 0.0).astype(bf16)

        def body(t, seen):
            tile = skey_ref[t]
            eq = tile == thr
            eqf = jnp.where(eq, 1.0, 0.0).astype(bf16)
            rank = jnp.dot(tri, eqf, preferred_element_type=f32) + seen
            skey_ref[t] = jnp.where(jnp.logical_and(eq, rank > need), INT_MIN, tile)
            return rank[KT - 1:KT, :]

        lax.fori_loop(0, n_tiles, body, jnp.zeros((1, TQ), f32))

    n_hq = N_HEADS_A * TQ
    g_hq = GQA_GROUP * TQ

    def logits(t):
        bias = jnp.where(skey_ref[t] >= thr, 0.0, -jnp.inf)
        bias = jnp.concatenate([bias] * N_HEADS_A, axis=1)
        return jnp.dot(k_ref[t], qT_ref[...], preferred_element_type=f32) + bias

    m8 = lax.fori_loop(0, n_tiles, lambda t, m: jnp.maximum(m, fold8(logits(t), jnp.max)),
                       jnp.full((SUBLANES, n_hq), -jnp.inf, f32))
    mrow = jnp.maximum(jnp.max(m8, axis=0, keepdims=True), NEG_BIG)

    def pv_tile(t, st):
        l8, acc0, acc1 = st
        pr = jnp.exp2(logits(t) - mrow)
        l8 = l8 + fold8(pr, jnp.sum)
        prb = pr.astype(bf16)
        acc0 = acc0 + jnp.dot(vT_ref[t, :HEAD_DIM, :], prb[:, :g_hq], preferred_element_type=f32)
        acc1 = acc1 + jnp.dot(vT_ref[t, HEAD_DIM:, :], prb[:, g_hq:], preferred_element_type=f32)
        return l8, acc0, acc1

    l8, acc0, acc1 = lax.fori_loop(
        0, n_tiles, pv_tile,
        (jnp.zeros((SUBLANES, n_hq), f32), jnp.zeros((HEAD_DIM, g_hq), f32), jnp.zeros((HEAD_DIM, g_hq), f32)))
    linv = 1.0 / jnp.sum(l8, axis=0, keepdims=True)
    oT = [acc0 * linv[:, :g_hq], acc1 * linv[:, g_hq:]]
    for j in range(D_A // LANES):
        g, hh = (2 * j) // GQA_GROUP, (2 * j) % GQA_GROUP
        pair = oT[g][:, hh * TQ:(hh + 2) * TQ]
        stacked = jnp.concatenate([pair[:, :TQ], pair[:, TQ:]], axis=0)
        ab_ref[:, j * LANES:(j + 1) * LANES] = stacked.T.astype(ab_ref.dtype)

    lane_c = lax.broadcasted_iota(i32, (GMLP_CHUNK, LANES), 1)
    for p2 in range(N_GROUPS_B // 2):
        xg = gvn_ref[:, p2 * LANES:(p2 + 1) * LANES].astype(bf16)
        both = jnp.dot(wm_ref[p2], xg, preferred_element_type=f32)
        mix = jnp.where(lane_c < GROUP_DIM_B, both[:GMLP_CHUNK], both[GMLP_CHUNK:])
        mix = mix + bias_ref[:, p2 * LANES:(p2 + 1) * LANES]
        gu = gu_ref[:, p2 * LANES:(p2 + 1) * LANES].astype(f32)
        ab_ref[:, D_A + p2 * LANES:D_A + (p2 + 1) * LANES] = (gu * mix).astype(ab_ref.dtype)


def _dsa_call(qT, qiT, wi, k, vT, ki, gvn, gu, wm, bias, *, causal, q_pos0, tq_real, n_keys, topk):
    nbatch, nblk = qT.shape[:2]
    n_kt = k.shape[1]
    kern = functools.partial(_dsa_kernel, causal=causal, q_pos0=q_pos0, tq_real=tq_real,
                             n_keys=n_keys, n_kt=n_kt, topk=topk)
    blk4 = lambda b, i: (b, i, 0, 0)
    per_b = lambda b, i: (b, 0, 0, 0)
    return pl.pallas_call(
        kern,
        grid=(nbatch, nblk),
        in_specs=[
            pl.BlockSpec((None, None, D_KV, N_HEADS_A * TQ), blk4),
            pl.BlockSpec((None, None, IDX_DIM, N_IDX_HEADS * TQ), blk4),
            pl.BlockSpec((None, None, N_IDX_HEADS, TQ), blk4),
            pl.BlockSpec((None, n_kt, KT, D_KV), per_b),
            pl.BlockSpec((None, n_kt, D_KV, KT), per_b),
            pl.BlockSpec((None, n_kt, KT, IDX_DIM), per_b),
            pl.BlockSpec((None, None, TQ, D_B), blk4),
            pl.BlockSpec((None, None, TQ, D_B), blk4),
            pl.BlockSpec((N_GROUPS_B // 2, 2 * GMLP_CHUNK, GMLP_CHUNK), lambda b, i: (0, 0, 0)),
            pl.BlockSpec((GMLP_CHUNK, D_B), lambda b, i: (0, 0)),
        ],
        out_specs=pl.BlockSpec((None, None, TQ, D_A + D_B), blk4),
        out_shape=jax.ShapeDtypeStruct((nbatch, nblk, TQ, D_A + D_B), jnp.bfloat16),
        scratch_shapes=[pltpu.VMEM((n_kt, KT, TQ), jnp.int32),
                        pltpu.VMEM((SUBLANES, TQ), jnp.int32)],
        compiler_params=pltpu.CompilerParams(
            dimension_semantics=("arbitrary", "arbitrary"), vmem_limit_bytes=VMEM_LIMIT),
        name="dsa_prompt" if causal else "dsa_sample",
    )(qT, qiT, wi, k, vT, ki, gvn, gu, wm, bias)


def _ffn_kernel(x_ref, ab_ref, wo_ref, g_ref, w1_ref, w2_ref, gf_ref, o_ref, *, final):
    f32 = jnp.float32
    x = x_ref[...] + jnp.dot(ab_ref[...], wo_ref[...], preferred_element_type=f32)
    ms = jnp.mean(x * x, axis=-1, keepdims=True)
    xn = (x * lax.rsqrt(ms + EPS) * g_ref[...]).astype(jnp.bfloat16)
    h = jnp.maximum(jnp.dot(xn, w1_ref[...], preferred_element_type=f32), 0.0)
    acc = x + jnp.dot((h * h).astype(jnp.bfloat16), w2_ref[...], preferred_element_type=f32)
    if final:
        ms2 = jnp.mean(acc * acc, axis=-1, keepdims=True)
        acc = acc * lax.rsqrt(ms2 + EPS) * gf_ref[...]
    o_ref[...] = acc


def _ffn_call(x, ab, wo, g, w1, w2, gf, *, final):
    rows = x.shape[0]
    row_map = lambda i: (i, 0)
    const_map = lambda i: (0, 0)
    once = pl.Buffered(1)
    return pl.pallas_call(
        functools.partial(_ffn_kernel, final=final),
        grid=(rows // TM,),
        in_specs=[
            pl.BlockSpec((TM, D_MODEL), row_map),
            pl.BlockSpec((TM, D_A + D_B), row_map),
            pl.BlockSpec((D_A + D_B, D_MODEL), const_map, pipeline_mode=once),
            pl.BlockSpec((1, D_MODEL), const_map),
            pl.BlockSpec((D_MODEL, D_FF), const_map, pipeline_mode=once),
            pl.BlockSpec((D_FF, D_MODEL), const_map, pipeline_mode=once),
            pl.BlockSpec((1, D_MODEL), const_map),
        ],
        out_specs=pl.BlockSpec((TM, D_MODEL), row_map),
        out_shape=jax.ShapeDtypeStruct((rows, D_MODEL), jnp.float32),
        compiler_params=pltpu.CompilerParams(
            dimension_semantics=("arbitrary",), vmem_limit_bytes=VMEM_LIMIT),
        name="ffn",
    )(x, ab, wo, g, w1, w2, gf)


def _rope_tables(pos):
    posf = pos.astype(jnp.float32)[:, None]
    lane = jnp.arange(LANES)

    def one(width, rot, limit):
        half = rot // 2
        d = lane % width
        inv = ROPE_THETA ** (-jnp.arange(half, dtype=jnp.float32) * (2.0 / rot))
        ang = posf * inv[None, :]
        cos_l = jnp.take(jnp.cos(ang), d % half, axis=1)
        sin_l = jnp.take(jnp.sin(ang), d % half, axis=1)
        live = lane < limit
        lo = jnp.logical_and(d < half, live)[None, :]
        hi = jnp.logical_and(jnp.logical_and(d >= half, d < rot), live)[None, :]
        c = jnp.where(jnp.logical_or(lo, hi), cos_l, 1.0)
        return [c, jnp.where(lo, -sin_l, 0.0), jnp.where(hi, sin_l, 0.0)]

    tabs = one(HEAD_DIM, ROT_DIM, LANES) + one(IDX_DIM, IDX_ROT, LANES) + one(IDX_DIM, IDX_ROT, IDX_DIM)
    return jnp.concatenate(tabs, axis=1)


def kernel(x_prompt, x_sample, cache_k, cache_v, cache_kidx, norm_mix, w_in, gate_norm, w_spatial,
           b_spatial, w_out, norm_ffn, w_ff1, w_ff2, norm_final):
    f32, bf16 = jnp.float32, jnp.bfloat16
    depth = w_in.shape[0]
    nb_p, t_p = x_prompt.shape[:2]
    nb_s, t_s = x_sample.shape[:2]
    past = cache_k.shape[2]
    rows_p, rows_s = nb_p * t_p, nb_s * TQ
    assert t_p % TM == 0 and rows_s % TM == 0 and t_p % KT == 0
    assert t_s * 2 == TQ and t_s == CHUNK and past % CHUNK == 0
    nblk_p = t_p // TQ
    topk_p = min(TOPK_MAX, t_p // 4)
    topk_s = min(TOPK_MAX, (past + t_s) // 4)
    s_all = past + t_s
    s_pad = -(-s_all // KT) * KT

    n_head_cols = _U0 - 88
    w_in_p = jnp.concatenate(
        [w_in[:, :, :n_head_cols], jnp.zeros((depth, D_MODEL, _U0 - n_head_cols), w_in.dtype),
         w_in[:, :, n_head_cols:]], axis=2).astype(bf16)
    w_out_b = w_out.astype(bf16)
    w1_b = w_ff1.astype(bf16)
    w2_b = w_ff2.astype(bf16)
    cm = (jnp.arange(GMLP_CHUNK)[None, :] // CHUNK <= jnp.arange(GMLP_CHUNK)[:, None] // CHUNK)
    wm_p = jnp.where(cm[None, None], w_spatial, 0.0).astype(bf16)
    small = (jnp.arange(GMLP_CHUNK) < t_s)
    wm_s = jnp.where(jnp.logical_and(small[:, None], small[None, :])[None, None], w_spatial, 0.0).astype(bf16)
    wm_p = wm_p.reshape(depth, N_GROUPS_B // 2, 2 * GMLP_CHUNK, GMLP_CHUNK)
    wm_s = wm_s.reshape(depth, N_GROUPS_B // 2, 2 * GMLP_CHUNK, GMLP_CHUNK)
    bias_sp = jnp.repeat(jnp.swapaxes(b_spatial, 1, 2), GROUP_DIM_B, axis=2)

    pos_s = past + jnp.arange(t_s, dtype=jnp.int32)
    pos_tab = jnp.concatenate([jnp.arange(t_p, dtype=jnp.int32),
                               jnp.tile(jnp.concatenate([pos_s, pos_s]), TM // TQ)])
    tab = _rope_tables(pos_tab)

    cache_kb = cache_k.astype(bf16).reshape(depth, nb_s, past, D_KV)
    cache_vb = cache_v.astype(bf16).reshape(depth, nb_s, past, D_KV)
    cache_kib = cache_kidx.astype(bf16)
    pad_rows = jnp.zeros((nb_s, s_pad - s_all, D_KV), bf16)

    x = jnp.concatenate([x_prompt.reshape(rows_p, D_MODEL),
                         jnp.concatenate([x_sample, x_sample], axis=1).reshape(rows_s, D_MODEL)], axis=0)

    def first_half(a):
        return a.reshape((nb_s, TQ) + a.shape[1:])[:, :t_s]

    kp_l, vp_l, kip_l, ks_l, vs_l, kis_l, gvs_l = [], [], [], [], [], [], []
    for l in range(depth):
        qT, qiT, wiT, k, v, ki, kb, vT, kib, gu, gvn = _proj_call(
            x, norm_mix[l][None, :], w_in_p[l], tab, gate_norm[l][None, :], rows_p // TM, t_p // TM)
        nq_p = rows_p // TQ

        ab_p = _dsa_call(
            qT[:nq_p].reshape((nb_p, nblk_p) + qT.shape[1:]),
            qiT[:nq_p].reshape((nb_p, nblk_p) + qiT.shape[1:]),
            wiT[:nq_p].reshape((nb_p, nblk_p) + wiT.shape[1:]),
            kb[:rows_p].reshape(nb_p, t_p // KT, KT, D_KV),
            vT[:rows_p // KT].reshape(nb_p, t_p // KT, D_KV, KT),
            kib[:rows_p].reshape(nb_p, t_p // KT, KT, IDX_DIM),
            gvn[:rows_p].reshape(nb_p, nblk_p, TQ, D_B), gu[:rows_p].reshape(nb_p, nblk_p, TQ, D_B),
            wm_p[l], bias_sp[l], causal=True, q_pos0=0, tq_real=TQ, n_keys=t_p, topk=topk_p)

        k_s = first_half(k[rows_p:])
        v_s = first_half(v[rows_p:])
        ki_s = first_half(ki[rows_p:])
        k_all = jnp.concatenate([cache_kb[l], first_half(kb[rows_p:]), pad_rows], axis=1)
        v_all = jnp.concatenate([cache_vb[l], v_s.astype(bf16), pad_rows], axis=1)
        ki_all = jnp.concatenate([cache_kib[l], first_half(kib[rows_p:]), pad_rows[:, :, :IDX_DIM]], axis=1)
        ab_s = _dsa_call(
            qT[nq_p:].reshape((nb_s, 1) + qT.shape[1:]),
            qiT[nq_p:].reshape((nb_s, 1) + qiT.shape[1:]),
            wiT[nq_p:].reshape((nb_s, 1) + wiT.shape[1:]),
            k_all.reshape(nb_s, s_pad // KT, KT, D_KV),
            jnp.swapaxes(v_all.reshape(nb_s, s_pad // KT, KT, D_KV), 2, 3),
            ki_all.reshape(nb_s, s_pad // KT, KT, IDX_DIM),
            gvn[rows_p:].reshape(nb_s, 1, TQ, D_B), gu[rows_p:].reshape(nb_s, 1, TQ, D_B),
            wm_s[l], bias_sp[l], causal=False, q_pos0=past, tq_real=t_s, n_keys=s_all, topk=topk_s)

        ab = jnp.concatenate([ab_p.reshape(rows_p, D_A + D_B), ab_s.reshape(rows_s, D_A + D_B)], axis=0)
        x = _ffn_call(x, ab, w_out_b[l], norm_ffn[l][None, :], w1_b[l], w2_b[l], norm_final[None, :],
                      final=(l == depth - 1))

        kp_l.append(k[:rows_p].reshape(nb_p, t_p, N_KV_HEADS, HEAD_DIM))
        vp_l.append(v[:rows_p].reshape(nb_p, t_p, N_KV_HEADS, HEAD_DIM))
        kip_l.append(ki[:rows_p].reshape(nb_p, t_p, IDX_DIM))
        ks_l.append(k_s.reshape(nb_s, t_s, N_KV_HEADS, HEAD_DIM))
        vs_l.append(v_s.reshape(nb_s, t_s, N_KV_HEADS, HEAD_DIM))
        kis_l.append(ki_s)
        gvs_l.append(first_half(gvn[rows_p:]))

    y_prompt = x[:rows_p].reshape(nb_p, t_p, D_MODEL)
    y_sample = first_half(x[rows_p:])
    return (y_prompt, y_sample, jnp.stack(kp_l), jnp.stack(vp_l), jnp.stack(kip_l),
            jnp.stack(ks_l), jnp.stack(vs_l), jnp.stack(kis_l), jnp.stack(gvs_l))
```

```python
import functools

import numpy as np
import jax
import jax.numpy as jnp
from jax import lax
from jax.experimental import pallas as pl
from jax.experimental.pallas import tpu as pltpu

D_MODEL = 1024
CHUNK = 64
D_A = 512
D_B = 512
HEAD_DIM = 64
N_HEADS_A = 8
N_KV_HEADS = 2
GQA_GROUP = 4
D_KV = 128
ROT_DIM = 16
N_IDX_HEADS = 8
IDX_DIM = 32
IDX_ROT = 8
TOPK_MAX = 256
N_GROUPS_B = 8
GROUP_DIM_B = 64
GMLP_CHUNK = 128
D_FF = 4096
ROPE_THETA = 500000.0
EPS = 1e-6

LANES = 128
SUBLANES = 8
PACKED = 16
N_IN_PAD = 2176
TM = 512
TQ = 128
KT = 512
VMEM_LIMIT = 56 * 1024 * 1024

INT_MIN = np.int32(-2147483648)
MANT_MASK = np.int32(0x7FFFFFFF)
HALF = 1 << 15
NEG_BIG = -1e30
LOG2E = 1.4426950408889634

_Q0, _K0, _V0, _QI0, _KW0, _U0, _GV0 = 0, 512, 640, 768, 1024, 1152, 1664


def _gelu(x):
    return x * (0.5 * (1.0 + jnp.tanh(0.7978845608028654 * (x + 0.044715 * (x * x * x)))))


def _rope128(x, c, s_lo, s_hi, half):
    return x * c + pltpu.roll(x, LANES - half, 1) * s_lo + pltpu.roll(x, half, 1) * s_hi


def _sortable(bits):
    return bits ^ ((bits >> 31) & MANT_MASK)


def _tree_sum(parts):
    while len(parts) > 1:
        parts = [parts[i] + parts[i + 1] for i in range(0, len(parts) - 1, 2)] + (
            [parts[-1]] if len(parts) % 2 else [])
    return parts[0]


def _proj_kernel(*refs, n_alias):
    (x_ref, g_ref, w_ref, tab_ref, gn_ref) = refs[:5]
    (qT_ref, qiT_ref, wi_ref, k_ref, v_ref, ki_ref, kb_ref, vT_ref, kib_ref, gu_ref, gvn_ref) = refs[5 + n_alias:]
    bf16 = jnp.bfloat16
    x = x_ref[...]
    ms = jnp.mean(x * x, axis=-1, keepdims=True)
    xn = (x * lax.rsqrt(ms + EPS) * g_ref[...]).astype(bf16)
    z = jnp.dot(xn, w_ref[...], preferred_element_type=jnp.float32)

    def tab(j):
        return tab_ref[:, j * LANES:(j + 1) * LANES]

    cq, sq_lo, sq_hi = tab(0), tab(1), tab(2)
    ci, si_lo, si_hi = tab(3), tab(4), tab(5)
    ck, sk_lo, sk_hi = tab(6), tab(7), tab(8)
    n_blk = TM // TQ

    def blocks_T(slab):
        return [slab[b * TQ:(b + 1) * TQ, :].T for b in range(n_blk)]

    qT_ref[...] = jnp.zeros(qT_ref.shape, qT_ref.dtype)
    for j in range(D_A // LANES):
        zq = _rope128(z[:, _Q0 + j * LANES:_Q0 + (j + 1) * LANES], cq, sq_lo, sq_hi, ROT_DIM // 2)
        zq = zq * (HEAD_DIM ** -0.5 * LOG2E)
        for b, t in enumerate(blocks_T(zq)):
            for r in range(LANES // HEAD_DIM):
                h = j * (LANES // HEAD_DIM) + r
                g = h // GQA_GROUP
                qT_ref[b, g * HEAD_DIM:(g + 1) * HEAD_DIM, h * TQ:(h + 1) * TQ] = (
                    t[r * HEAD_DIM:(r + 1) * HEAD_DIM, :].astype(bf16))

    kk = _rope128(z[:, _K0:_K0 + LANES], cq, sq_lo, sq_hi, ROT_DIM // 2)
    k_ref[...] = kk
    kb_ref[...] = kk.astype(bf16)
    vv = z[:, _V0:_V0 + LANES]
    v_ref[...] = vv
    for b, t in enumerate(blocks_T(vv)):
        kt, off = (b * TQ) // KT, (b * TQ) % KT
        vT_ref[kt, :, off:off + TQ] = t.astype(bf16)

    for j in range(N_IDX_HEADS * IDX_DIM // LANES):
        zi = _rope128(z[:, _QI0 + j * LANES:_QI0 + (j + 1) * LANES], ci, si_lo, si_hi, IDX_ROT // 2)
        for b, t in enumerate(blocks_T(zi)):
            for r in range(LANES // IDX_DIM):
                h = j * (LANES // IDX_DIM) + r
                qiT_ref[b, :, h * TQ:(h + 1) * TQ] = t[r * IDX_DIM:(r + 1) * IDX_DIM, :].astype(bf16)

    kw = _rope128(z[:, _KW0:_KW0 + LANES], ck, sk_lo, sk_hi, IDX_ROT // 2)
    ki_ref[...] = kw[:, :IDX_DIM]
    kib_ref[...] = kw[:, :IDX_DIM].astype(bf16)
    for b, t in enumerate(blocks_T(kw)):
        wi_ref[b] = t[IDX_DIM:IDX_DIM + N_IDX_HEADS, :] * (IDX_DIM ** -0.5 * N_IDX_HEADS ** -0.5)

    gu_ref[...] = _gelu(z[:, _U0:_U0 + D_B]).astype(gu_ref.dtype)
    gg = _gelu(z[:, _GV0:_GV0 + D_B])
    gms = jnp.mean(gg * gg, axis=-1, keepdims=True)
    gvn_ref[...] = (gg * lax.rsqrt(gms + EPS) * gn_ref[...]).astype(gvn_ref.dtype)


def _proj_call(x, g, w, tab, gn, layer, depth, stacks, gvn_dtype):
    rows = x.shape[0]
    n_tiles = rows // TM
    n_pos_tiles = tab.shape[0] // TM
    n_blk = TM // TQ
    f32, bf16 = jnp.float32, jnp.bfloat16
    n_alias = 0 if stacks is None else 3

    row2 = lambda i: (i, 0)
    const2 = lambda i: (0, 0)
    lead3 = lambda i: (i, 0, 0)
    lay3 = lambda i: (layer, i, 0)
    outs = [
        ((rows // TQ, D_KV, N_HEADS_A * TQ), bf16, (n_blk, D_KV, N_HEADS_A * TQ), lead3),
        ((rows // TQ, IDX_DIM, N_IDX_HEADS * TQ), bf16, (n_blk, IDX_DIM, N_IDX_HEADS * TQ), lead3),
        ((rows // TQ, N_IDX_HEADS, TQ), f32, (n_blk, N_IDX_HEADS, TQ), lead3),
        ((depth, rows, D_KV), f32, (None, TM, D_KV), lay3),
        ((depth, rows, D_KV), f32, (None, TM, D_KV), lay3),
        ((depth, rows, IDX_DIM), f32, (None, TM, IDX_DIM), lay3),
        ((rows, D_KV), bf16, (TM, D_KV), row2),
        ((rows // KT, D_KV, KT), bf16, (TM // KT, D_KV, KT), lead3),
        ((rows, IDX_DIM), bf16, (TM, IDX_DIM), row2),
        ((rows, D_B), bf16, (TM, D_B), row2),
        ((rows, D_B), gvn_dtype, (TM, D_B), row2),
    ]
    in_specs = [
        pl.BlockSpec((TM, D_MODEL), row2),
        pl.BlockSpec((1, D_MODEL), const2),
        pl.BlockSpec((D_MODEL, N_IN_PAD), const2),
        pl.BlockSpec((TM, 9 * LANES), lambda i: (i % n_pos_tiles, 0)),
        pl.BlockSpec((1, D_B), const2),
    ] + [pl.BlockSpec(memory_space=pl.ANY)] * n_alias
    args = (x, g, w, tab, gn) + (() if stacks is None else tuple(stacks))
    return pl.pallas_call(
        functools.partial(_proj_kernel, n_alias=n_alias),
        grid=(n_tiles,),
        in_specs=in_specs,
        out_specs=tuple(pl.BlockSpec(blk, imap) for _, _, blk, imap in outs),
        out_shape=tuple(jax.ShapeDtypeStruct(shape, dt) for shape, dt, _, _ in outs),
        input_output_aliases={5 + j: 3 + j for j in range(n_alias)},
        compiler_params=pltpu.CompilerParams(
            dimension_semantics=("arbitrary",), vmem_limit_bytes=VMEM_LIMIT),
        name="proj",
    )(*args)


def _dsa_kernel(qT_ref, qiT_ref, wi_ref, k_ref, vT_ref, ki_ref, gvn_ref, gu_ref, wm_ref, bias_ref,
                ab_ref, skey_ref, dig_ref, *, causal, q_pos0, tq_real, n_keys, n_kt, topk):
    f32, bf16, i32, i16 = jnp.float32, jnp.bfloat16, jnp.int32, jnp.int16
    i = pl.program_id(1)
    if causal:
        n_tiles = ((i + 1) * TQ + KT - 1) // KT
        q0 = i * TQ
    else:
        n_tiles = n_kt
        q0 = q_pos0
    lane = lax.broadcasted_iota(i32, (1, TQ), 1)
    qchunk = (q0 + (lane & (tq_real - 1))) // CHUNK
    n_adm = jnp.minimum((qchunk + 1) * CHUNK, n_keys)

    def fold8(a, op):
        return op(a.reshape(a.shape[0] // SUBLANES, SUBLANES, a.shape[1]), axis=0)

    def score_tile(t, carry):
        ki_t = ki_ref[t]
        r = jnp.maximum(jnp.dot(ki_t, qiT_ref[...], preferred_element_type=f32), 0.0)
        acc = r[:, :TQ] * wi_ref[0:1, :]
        for h in range(1, N_IDX_HEADS):
            acc = acc + r[:, h * TQ:(h + 1) * TQ] * wi_ref[h:h + 1, :]
        kpos = t * KT + lax.broadcasted_iota(i32, (KT, TQ), 0)
        adm = (kpos // CHUNK) <= qchunk
        acc = jnp.where(acc == 0.0, 0.0, acc)
        mkey = jnp.where(adm, _sortable(pltpu.bitcast(acc, i32)), INT_MIN)
        skey_ref[t] = mkey
        dig_ref[t] = (mkey >> 16).astype(i16)
        return carry

    lax.fori_loop(0, n_tiles, score_tile, 0)

    def count16(c_signed):
        cb = jnp.broadcast_to(c_signed, (PACKED, TQ)).astype(i16)

        def body(t, acc):
            d = dig_ref[t].reshape(KT // PACKED, PACKED, TQ)
            hit = jnp.where(d >= cb, jnp.int16(1), jnp.int16(0))
            return acc + _tree_sum([hit[j] for j in range(KT // PACKED)])

        acc = lax.fori_loop(0, n_tiles, body, jnp.zeros((PACKED, TQ), i16))
        return jnp.sum(acc.astype(i32), axis=0, keepdims=True)

    def digit_search(base):
        def bit_body(j, st):
            u, cnt_at = st
            cu = u | jnp.left_shift(jnp.int32(1), 15 - j)
            cnt = base + count16(cu - HALF)
            ok = cnt >= topk
            return jnp.where(ok, cu, u), jnp.where(ok, cnt, cnt_at)

        zero = jnp.zeros((1, TQ), i32)
        return lax.fori_loop(0, 16, bit_body, (zero, zero))

    zero = jnp.zeros((1, TQ), i32)
    hi_u, n_ge_hi = digit_search(zero)
    hi_s = hi_u - HALF
    n_gt_hi = jnp.where(hi_u == 2 * HALF - 1, 0, count16(jnp.minimum(hi_s + 1, HALF - 1)))

    def low_digit_tile(t, carry):
        mk = skey_ref[t]
        dig_ref[t] = jnp.where((mk >> 16) == hi_s, (mk & 0xFFFF) - HALF, -HALF).astype(i16)
        return carry

    lax.fori_loop(0, n_tiles, low_digit_tile, 0)
    lo_u, n_ge_lo = digit_search(n_gt_hi)
    few = n_adm <= topk
    thr = jnp.where(few, INT_MIN + 1, hi_s * (2 * HALF) + lo_u)
    cge = jnp.where(few, 0, jnp.where(lo_u > 0, n_ge_lo, n_ge_hi))

    def count(pred_fn):
        def body(t, acc8):
            return acc8 + fold8(jnp.where(pred_fn(skey_ref[t]), 1, 0).astype(i32), jnp.sum)
        acc8 = lax.fori_loop(0, n_tiles, body, jnp.zeros((SUBLANES, TQ), i32))
        return jnp.sum(acc8, axis=0, keepdims=True)

    @pl.when(jnp.max(jnp.where(cge > topk, 1.0, 0.0)) > 0)
    def _():
        need = (topk - count(lambda tile: tile > thr)).astype(f32)
        rr = lax.broadcasted_iota(i32, (KT, KT), 0)
        cc = lax.broadcasted_iota(i32, (KT, KT), 1)
        tri = jnp.where(cc <= rr, 1.0, 0.0).astype(bf16)

        def body(t, seen):
            tile = skey_ref[t]
            eq = tile == thr
            eqf = jnp.where(eq, 1.0, 0.0).astype(bf16)
            rank = jnp.dot(tri, eqf, preferred_element_type=f32) + seen
            skey_ref[t] = jnp.where(jnp.logical_and(eq, rank > need), INT_MIN, tile)
            return rank[KT - 1:KT, :]

        lax.fori_loop(0, n_tiles, body, jnp.zeros((1, TQ), f32))

    n_hq = N_HEADS_A * TQ
    g_hq = GQA_GROUP * TQ

    def logits(t):
        bias = jnp.where(skey_ref[t] >= thr, 0.0, -jnp.inf)
        bias = jnp.concatenate([bias] * N_HEADS_A, axis=1)
        return jnp.dot(k_ref[t], qT_ref[...], preferred_element_type=f32) + bias

    m8 = lax.fori_loop(0, n_tiles, lambda t, m: jnp.maximum(m, fold8(logits(t), jnp.max)),
                       jnp.full((SUBLANES, n_hq), -jnp.inf, f32))
    mrow = jnp.maximum(jnp.max(m8, axis=0, keepdims=True), NEG_BIG)

    def pv_tile(t, st):
        l8, acc0, acc1 = st
        pr = jnp.exp2(logits(t) - mrow)
        l8 = l8 + fold8(pr, jnp.sum)
        prb = pr.astype(bf16)
        acc0 = acc0 + jnp.dot(vT_ref[t, :HEAD_DIM, :], prb[:, :g_hq], preferred_element_type=f32)
        acc1 = acc1 + jnp.dot(vT_ref[t, HEAD_DIM:, :], prb[:, g_hq:], preferred_element_type=f32)
        return l8, acc0, acc1

    l8, acc0, acc1 = lax.fori_loop(
        0, n_tiles, pv_tile,
        (jnp.zeros((SUBLANES, n_hq), f32), jnp.zeros((HEAD_DIM, g_hq), f32), jnp.zeros((HEAD_DIM, g_hq), f32)))
    linv = 1.0 / jnp.sum(l8, axis=0, keepdims=True)
    oT = [acc0 * linv[:, :g_hq], acc1 * linv[:, g_hq:]]
    for j in range(D_A // LANES):
        g, hh = (2 * j) // GQA_GROUP, (2 * j) % GQA_GROUP
        pair = oT[g][:, hh * TQ:(hh + 2) * TQ]
        stacked = jnp.concatenate([pair[:, :TQ], pair[:, TQ:]], axis=0)
        ab_ref[:, j * LANES:(j + 1) * LANES] = stacked.T.astype(ab_ref.dtype)

    lane_c = lax.broadcasted_iota(i32, (GMLP_CHUNK, LANES), 1)
    for p2 in range(N_GROUPS_B // 2):
        xg = gvn_ref[:, p2 * LANES:(p2 + 1) * LANES].astype(bf16)
        both = jnp.dot(wm_ref[p2], xg, preferred_element_type=f32)
        mix = jnp.where(lane_c < GROUP_DIM_B, both[:GMLP_CHUNK], both[GMLP_CHUNK:])
        mix = mix + bias_ref[:, p2 * LANES:(p2 + 1) * LANES]
        gu = gu_ref[:, p2 * LANES:(p2 + 1) * LANES].astype(f32)
        ab_ref[:, D_A + p2 * LANES:D_A + (p2 + 1) * LANES] = (gu * mix).astype(ab_ref.dtype)


def _dsa_call(qT, qiT, wi, k, vT, ki, gvn, gu, wm, bias, *, causal, q_pos0, tq_real, n_keys, topk):
    nbatch, nblk = qT.shape[:2]
    n_kt = k.shape[1]
    kern = functools.partial(_dsa_kernel, causal=causal, q_pos0=q_pos0, tq_real=tq_real,
                             n_keys=n_keys, n_kt=n_kt, topk=topk)
    blk4 = lambda b, i: (b, i, 0, 0)
    per_b = lambda b, i: (b, 0, 0, 0)
    return pl.pallas_call(
        kern,
        grid=(nbatch, nblk),
        in_specs=[
            pl.BlockSpec((None, None, D_KV, N_HEADS_A * TQ), blk4),
            pl.BlockSpec((None, None, IDX_DIM, N_IDX_HEADS * TQ), blk4),
            pl.BlockSpec((None, None, N_IDX_HEADS, TQ), blk4),
            pl.BlockSpec((None, n_kt, KT, D_KV), per_b),
            pl.BlockSpec((None, n_kt, D_KV, KT), per_b),
            pl.BlockSpec((None, n_kt, KT, IDX_DIM), per_b),
            pl.BlockSpec((None, None, TQ, D_B), blk4),
            pl.BlockSpec((None, None, TQ, D_B), blk4),
            pl.BlockSpec((N_GROUPS_B // 2, 2 * GMLP_CHUNK, GMLP_CHUNK), lambda b, i: (0, 0, 0)),
            pl.BlockSpec((GMLP_CHUNK, D_B), lambda b, i: (0, 0)),
        ],
        out_specs=pl.BlockSpec((None, None, TQ, D_A + D_B), blk4),
        out_shape=jax.ShapeDtypeStruct((nbatch, nblk, TQ, D_A + D_B), jnp.bfloat16),
        scratch_shapes=[pltpu.VMEM((n_kt, KT, TQ), jnp.int32),
                        pltpu.VMEM((n_kt, KT, TQ), jnp.int16)],
        compiler_params=pltpu.CompilerParams(
            dimension_semantics=("arbitrary", "arbitrary"), vmem_limit_bytes=VMEM_LIMIT),
        name="dsa_prompt" if causal else "dsa_sample",
    )(qT, qiT, wi, k, vT, ki, gvn, gu, wm, bias)


def _ffn_kernel(x_ref, ab_ref, wo_ref, g_ref, w1_ref, w2_ref, gf_ref, o_ref, *, final):
    f32 = jnp.float32
    x = x_ref[...] + jnp.dot(ab_ref[...], wo_ref[...], preferred_element_type=f32)
    ms = jnp.mean(x * x, axis=-1, keepdims=True)
    xn = (x * lax.rsqrt(ms + EPS) * g_ref[...]).astype(jnp.bfloat16)
    h = jnp.maximum(jnp.dot(xn, w1_ref[...], preferred_element_type=f32), 0.0)
    acc = x + jnp.dot((h * h).astype(jnp.bfloat16), w2_ref[...], preferred_element_type=f32)
    if final:
        ms2 = jnp.mean(acc * acc, axis=-1, keepdims=True)
        acc = acc * lax.rsqrt(ms2 + EPS) * gf_ref[...]
    o_ref[...] = acc


def _ffn_call(x, ab, wo, g, w1, w2, gf, *, final):
    rows = x.shape[0]
    row_map = lambda i: (i, 0)
    const_map = lambda i: (0, 0)
    once = pl.Buffered(1)
    return pl.pallas_call(
        functools.partial(_ffn_kernel, final=final),
        grid=(rows // TM,),
        in_specs=[
            pl.BlockSpec((TM, D_MODEL), row_map),
            pl.BlockSpec((TM, D_A + D_B), row_map),
            pl.BlockSpec((D_A + D_B, D_MODEL), const_map, pipeline_mode=once),
            pl.BlockSpec((1, D_MODEL), const_map),
            pl.BlockSpec((D_MODEL, D_FF), const_map, pipeline_mode=once),
            pl.BlockSpec((D_FF, D_MODEL), const_map, pipeline_mode=once),
            pl.BlockSpec((1, D_MODEL), const_map),
        ],
        out_specs=pl.BlockSpec((TM, D_MODEL), row_map),
        out_shape=jax.ShapeDtypeStruct((rows, D_MODEL), jnp.float32),
        compiler_params=pltpu.CompilerParams(
            dimension_semantics=("arbitrary",), vmem_limit_bytes=VMEM_LIMIT),
        name="ffn",
    )(x, ab, wo, g, w1, w2, gf)


def _rope_tables(pos):
    posf = pos.astype(jnp.float32)[:, None]
    lane = jnp.arange(LANES)

    def one(width, rot, limit):
        half = rot // 2
        d = lane % width
        inv = ROPE_THETA ** (-jnp.arange(half, dtype=jnp.float32) * (2.0 / rot))
        ang = posf * inv[None, :]
        cos_l = jnp.take(jnp.cos(ang), d % half, axis=1)
        sin_l = jnp.take(jnp.sin(ang), d % half, axis=1)
        live = lane < limit
        lo = jnp.logical_and(d < half, live)[None, :]
        hi = jnp.logical_and(jnp.logical_and(d >= half, d < rot), live)[None, :]
        c = jnp.where(jnp.logical_or(lo, hi), cos_l, 1.0)
        return [c, jnp.where(lo, -sin_l, 0.0), jnp.where(hi, sin_l, 0.0)]

    tabs = one(HEAD_DIM, ROT_DIM, LANES) + one(IDX_DIM, IDX_ROT, LANES) + one(IDX_DIM, IDX_ROT, IDX_DIM)
    return jnp.concatenate(tabs, axis=1)


def kernel(x_prompt, x_sample, cache_k, cache_v, cache_kidx, norm_mix, w_in, gate_norm, w_spatial,
           b_spatial, w_out, norm_ffn, w_ff1, w_ff2, norm_final):
    f32, bf16 = jnp.float32, jnp.bfloat16
    depth = w_in.shape[0]
    nb_p, t_p = x_prompt.shape[:2]
    nb_s, t_s = x_sample.shape[:2]
    past = cache_k.shape[2]
    rows_p, rows_s = nb_p * t_p, nb_s * TQ
    assert t_p % TM == 0 and rows_s % TM == 0 and t_p % KT == 0
    assert t_s * 2 == TQ and t_s == CHUNK and past % CHUNK == 0
    nblk_p = t_p // TQ
    nkt_p = t_p // KT
    topk_p = min(TOPK_MAX, t_p // 4)
    topk_s = min(TOPK_MAX, (past + t_s) // 4)
    s_all = past + t_s
    s_pad = -(-s_all // KT) * KT
    nkt_s = s_pad // KT

    n_head_cols = _U0 - 88
    w_in_p = jnp.concatenate(
        [w_in[:, :, :n_head_cols], jnp.zeros((depth, D_MODEL, _U0 - n_head_cols), w_in.dtype),
         w_in[:, :, n_head_cols:]], axis=2).astype(bf16)
    w_out_b = w_out.astype(bf16)
    w1_b = w_ff1.astype(bf16)
    w2_b = w_ff2.astype(bf16)
    cm = (jnp.arange(GMLP_CHUNK)[None, :] // CHUNK <= jnp.arange(GMLP_CHUNK)[:, None] // CHUNK)
    wm_p = jnp.where(cm[None, None], w_spatial, 0.0).astype(bf16)
    small = (jnp.arange(GMLP_CHUNK) < t_s)
    wm_s = jnp.where(jnp.logical_and(small[:, None], small[None, :])[None, None], w_spatial, 0.0).astype(bf16)
    wm_p = wm_p.reshape(depth, N_GROUPS_B // 2, 2 * GMLP_CHUNK, GMLP_CHUNK)
    wm_s = wm_s.reshape(depth, N_GROUPS_B // 2, 2 * GMLP_CHUNK, GMLP_CHUNK)
    bias_sp = jnp.repeat(jnp.swapaxes(b_spatial, 1, 2), GROUP_DIM_B, axis=2)

    pos_s = past + jnp.arange(t_s, dtype=jnp.int32)
    tab_p = _rope_tables(jnp.arange(t_p, dtype=jnp.int32))
    tab_s = _rope_tables(jnp.tile(jnp.concatenate([pos_s, pos_s]), TM // TQ))

    cache_kb = cache_k.astype(bf16).reshape(depth, nb_s, past, D_KV)
    cache_vb = cache_v.astype(bf16).reshape(depth, nb_s, past, D_KV)
    cache_kib = cache_kidx.astype(bf16)
    pad_rows = jnp.zeros((nb_s, s_pad - s_all, D_KV), bf16)

    xp = x_prompt.reshape(rows_p, D_MODEL)
    xs = jnp.concatenate([x_sample, x_sample], axis=1).reshape(rows_s, D_MODEL)

    def first_half(a):
        return a.reshape(a.shape[:-2] + (nb_s, TQ, a.shape[-1]))[..., :t_s, :]

    stk_p = stk_s = None
    gvs_l = []
    for l in range(depth):
        g_mix, g_gate, g_ffn = norm_mix[l][None, :], gate_norm[l][None, :], norm_ffn[l][None, :]
        final = l == depth - 1

        qT, qiT, wiT, k_st, v_st, ki_st, kb, vT, kib, gu, gvn = _proj_call(
            xp, g_mix, w_in_p[l], tab_p, g_gate, l, depth, stk_p, bf16)
        stk_p = (k_st, v_st, ki_st)
        ab = _dsa_call(
            qT.reshape((nb_p, nblk_p) + qT.shape[1:]), qiT.reshape((nb_p, nblk_p) + qiT.shape[1:]),
            wiT.reshape((nb_p, nblk_p) + wiT.shape[1:]),
            kb.reshape(nb_p, nkt_p, KT, D_KV), vT.reshape(nb_p, nkt_p, D_KV, KT),
            kib.reshape(nb_p, nkt_p, KT, IDX_DIM),
            gvn.reshape(nb_p, nblk_p, TQ, D_B), gu.reshape(nb_p, nblk_p, TQ, D_B),
            wm_p[l], bias_sp[l], causal=True, q_pos0=0, tq_real=TQ, n_keys=t_p, topk=topk_p)
        xp = _ffn_call(xp, ab.reshape(rows_p, D_A + D_B), w_out_b[l], g_ffn, w1_b[l], w2_b[l],
                       norm_final[None, :], final=final)

        qT, qiT, wiT, k_st, v_st, ki_st, kb, vT, kib, gu, gvn = _proj_call(
            xs, g_mix, w_in_p[l], tab_s, g_gate, l, depth, stk_s, f32)
        stk_s = (k_st, v_st, ki_st)
        k_all = jnp.concatenate([cache_kb[l], first_half(kb), pad_rows], axis=1)
        v_all = jnp.concatenate([cache_vb[l], first_half(v_st[l]).astype(bf16), pad_rows], axis=1)
        ki_all = jnp.concatenate([cache_kib[l], first_half(kib), pad_rows[:, :, :IDX_DIM]], axis=1)
        ab = _dsa_call(
            qT.reshape((nb_s, 1) + qT.shape[1:]), qiT.reshape((nb_s, 1) + qiT.shape[1:]),
            wiT.reshape((nb_s, 1) + wiT.shape[1:]),
            k_all.reshape(nb_s, nkt_s, KT, D_KV),
            jnp.swapaxes(v_all.reshape(nb_s, nkt_s, KT, D_KV), 2, 3),
            ki_all.reshape(nb_s, nkt_s, KT, IDX_DIM),
            gvn.reshape(nb_s, 1, TQ, D_B), gu.reshape(nb_s, 1, TQ, D_B),
            wm_s[l], bias_sp[l], causal=False, q_pos0=past, tq_real=t_s, n_keys=s_all, topk=topk_s)
        xs = _ffn_call(xs, ab.reshape(rows_s, D_A + D_B), w_out_b[l], g_ffn, w1_b[l], w2_b[l],
                       norm_final[None, :], final=final)
        gvs_l.append(first_half(gvn))

    k_p, v_p, ki_p = stk_p
    k_s, v_s, ki_s = (first_half(a) for a in stk_s)
    return (xp.reshape(nb_p, t_p, D_MODEL), first_half(xs),
            k_p.reshape(depth, nb_p, t_p, N_KV_HEADS, HEAD_DIM),
            v_p.reshape(depth, nb_p, t_p, N_KV_HEADS, HEAD_DIM),
            ki_p.reshape(depth, nb_p, t_p, IDX_DIM),
            k_s.reshape(depth, nb_s, t_s, N_KV_HEADS, HEAD_DIM),
            v_s.reshape(depth, nb_s, t_s, N_KV_HEADS, HEAD_DIM),
            ki_s, jnp.stack(gvs_l))
```

```python
import functools

import numpy as np
import jax
import jax.numpy as jnp
from jax import lax
from jax.experimental import pallas as pl
from jax.experimental.pallas import tpu as pltpu

D_MODEL = 1024
CHUNK = 64
D_A = 512
D_B = 512
HEAD_DIM = 64
N_HEADS_A = 8
N_KV_HEADS = 2
GQA_GROUP = 4
D_KV = 128
ROT_DIM = 16
N_IDX_HEADS = 8
IDX_DIM = 32
IDX_ROT = 8
TOPK_MAX = 256
N_GROUPS_B = 8
GROUP_DIM_B = 64
GMLP_CHUNK = 128
D_FF = 4096
ROPE_THETA = 500000.0
EPS = 1e-6

LANES = 128
SUBLANES = 8
N_IN_PAD = 2176
TM = 512
TQ = 128
KT = 512
VMEM_LIMIT = 56 * 1024 * 1024

INT_MIN = np.int32(-2147483648)
MANT_MASK = np.int32(0x7FFFFFFF)
FAST_SEARCH_ITERS = 40
BOUND_SLACK = 1.02
L_TINY = 1e-30
NEG_BIG = -1e30
LOG2E = 1.4426950408889634

_Q0, _K0, _V0, _QI0, _KW0, _U0, _GV0 = 0, 512, 640, 768, 1024, 1152, 1664


def _gelu(x):
    return x * (0.5 * (1.0 + jnp.tanh(0.7978845608028654 * (x + 0.044715 * (x * x * x)))))


def _rope128(x, c, s_lo, s_hi, half):
    return x * c + pltpu.roll(x, LANES - half, 1) * s_lo + pltpu.roll(x, half, 1) * s_hi


def _sortable(bits):
    return bits ^ ((bits >> 31) & MANT_MASK)


def _proj_kernel(*refs, n_alias):
    (x_ref, g_ref, w_ref, tab_ref, gn_ref) = refs[:5]
    (qT_ref, qiT_ref, wi_ref, k_ref, v_ref, ki_ref, kb_ref, vT_ref, kib_ref, gu_ref, gvn_ref) = refs[5 + n_alias:]
    bf16 = jnp.bfloat16
    x = x_ref[...]
    ms = jnp.mean(x * x, axis=-1, keepdims=True)
    xn = (x * lax.rsqrt(ms + EPS) * g_ref[...]).astype(bf16)
    z = jnp.dot(xn, w_ref[...], preferred_element_type=jnp.float32)

    def tab(j):
        return tab_ref[:, j * LANES:(j + 1) * LANES]

    cq, sq_lo, sq_hi = tab(0), tab(1), tab(2)
    ci, si_lo, si_hi = tab(3), tab(4), tab(5)
    ck, sk_lo, sk_hi = tab(6), tab(7), tab(8)
    n_blk = TM // TQ

    def blocks_T(slab):
        return [slab[b * TQ:(b + 1) * TQ, :].T for b in range(n_blk)]

    qT_ref[...] = jnp.zeros(qT_ref.shape, qT_ref.dtype)
    for j in range(D_A // LANES):
        zq = _rope128(z[:, _Q0 + j * LANES:_Q0 + (j + 1) * LANES], cq, sq_lo, sq_hi, ROT_DIM // 2)
        zq = zq * (HEAD_DIM ** -0.5 * LOG2E)
        for b, t in enumerate(blocks_T(zq)):
            for r in range(LANES // HEAD_DIM):
                h = j * (LANES // HEAD_DIM) + r
                g = h // GQA_GROUP
                qT_ref[b, g * HEAD_DIM:(g + 1) * HEAD_DIM, h * TQ:(h + 1) * TQ] = (
                    t[r * HEAD_DIM:(r + 1) * HEAD_DIM, :].astype(bf16))

    kk = _rope128(z[:, _K0:_K0 + LANES], cq, sq_lo, sq_hi, ROT_DIM // 2)
    for g in range(N_KV_HEADS):
        k_ref[:, g, :] = kk[:, g * HEAD_DIM:(g + 1) * HEAD_DIM]
    kb_ref[...] = kk.astype(bf16)
    vv = z[:, _V0:_V0 + LANES]
    for g in range(N_KV_HEADS):
        v_ref[:, g, :] = vv[:, g * HEAD_DIM:(g + 1) * HEAD_DIM]
    for b, t in enumerate(blocks_T(vv)):
        kt, off = (b * TQ) // KT, (b * TQ) % KT
        vT_ref[kt, :, off:off + TQ] = t.astype(bf16)

    for j in range(N_IDX_HEADS * IDX_DIM // LANES):
        zi = _rope128(z[:, _QI0 + j * LANES:_QI0 + (j + 1) * LANES], ci, si_lo, si_hi, IDX_ROT // 2)
        for b, t in enumerate(blocks_T(zi)):
            for r in range(LANES // IDX_DIM):
                h = j * (LANES // IDX_DIM) + r
                qiT_ref[b, :, h * TQ:(h + 1) * TQ] = t[r * IDX_DIM:(r + 1) * IDX_DIM, :].astype(bf16)

    kw = _rope128(z[:, _KW0:_KW0 + LANES], ck, sk_lo, sk_hi, IDX_ROT // 2)
    ki_ref[...] = kw[:, :IDX_DIM]
    kib_ref[...] = kw[:, :IDX_DIM].astype(bf16)
    for b, t in enumerate(blocks_T(kw)):
        wi_ref[b] = t[IDX_DIM:IDX_DIM + N_IDX_HEADS, :] * (IDX_DIM ** -0.5 * N_IDX_HEADS ** -0.5)

    gu_ref[...] = _gelu(z[:, _U0:_U0 + D_B]).astype(gu_ref.dtype)
    gg = _gelu(z[:, _GV0:_GV0 + D_B])
    gms = jnp.mean(gg * gg, axis=-1, keepdims=True)
    gvn_ref[...] = (gg * lax.rsqrt(gms + EPS) * gn_ref[...]).astype(gvn_ref.dtype)


def _proj_call(x, g, w, tab, gn, layer, depth, stacks, gvn_dtype):
    rows = x.shape[0]
    n_tiles = rows // TM
    n_pos_tiles = tab.shape[0] // TM
    n_blk = TM // TQ
    f32, bf16 = jnp.float32, jnp.bfloat16
    n_alias = 0 if stacks is None else 3

    row2 = lambda i: (i, 0)
    const2 = lambda i: (0, 0)
    lead3 = lambda i: (i, 0, 0)
    lay3 = lambda i: (layer, i, 0)
    lay4 = lambda i: (layer, i, 0, 0)
    outs = [
        ((rows // TQ, D_KV, N_HEADS_A * TQ), bf16, (n_blk, D_KV, N_HEADS_A * TQ), lead3),
        ((rows // TQ, IDX_DIM, N_IDX_HEADS * TQ), bf16, (n_blk, IDX_DIM, N_IDX_HEADS * TQ), lead3),
        ((rows // TQ, N_IDX_HEADS, TQ), f32, (n_blk, N_IDX_HEADS, TQ), lead3),
        ((depth, rows, N_KV_HEADS, HEAD_DIM), f32, (None, TM, N_KV_HEADS, HEAD_DIM), lay4),
        ((depth, rows, N_KV_HEADS, HEAD_DIM), f32, (None, TM, N_KV_HEADS, HEAD_DIM), lay4),
        ((depth, rows, IDX_DIM), f32, (None, TM, IDX_DIM), lay3),
        ((rows, D_KV), bf16, (TM, D_KV), row2),
        ((rows // KT, D_KV, KT), bf16, (TM // KT, D_KV, KT), lead3),
        ((rows, IDX_DIM), bf16, (TM, IDX_DIM), row2),
        ((rows, D_B), bf16, (TM, D_B), row2),
        ((rows, D_B), gvn_dtype, (TM, D_B), row2),
    ]
    in_specs = [
        pl.BlockSpec((TM, D_MODEL), row2),
        pl.BlockSpec((1, D_MODEL), const2),
        pl.BlockSpec((D_MODEL, N_IN_PAD), const2),
        pl.BlockSpec((TM, 9 * LANES), lambda i: (i % n_pos_tiles, 0)),
        pl.BlockSpec((1, D_B), const2),
    ] + [pl.BlockSpec(memory_space=pl.ANY)] * n_alias
    args = (x, g, w, tab, gn) + (() if stacks is None else tuple(stacks))
    return pl.pallas_call(
        functools.partial(_proj_kernel, n_alias=n_alias),
        grid=(n_tiles,),
        in_specs=in_specs,
        out_specs=tuple(pl.BlockSpec(blk, imap) for _, _, blk, imap in outs),
        out_shape=tuple(jax.ShapeDtypeStruct(shape, dt) for shape, dt, _, _ in outs),
        input_output_aliases={5 + j: 3 + j for j in range(n_alias)},
        compiler_params=pltpu.CompilerParams(
            dimension_semantics=("arbitrary",), vmem_limit_bytes=VMEM_LIMIT),
        name="proj",
    )(*args)


def _dsa_kernel(qT_ref, qiT_ref, wi_ref, k_ref, vT_ref, ki_ref, gvn_ref, gu_ref, wm_ref, bias_ref,
                ab_ref, skey_ref, sel_ref, kn_ref, *, causal, q_pos0, tq_real, n_keys, n_kt, topk):
    f32, bf16, i32 = jnp.float32, jnp.bfloat16, jnp.int32
    i = pl.program_id(1)
    if causal:
        n_tiles = ((i + 1) * TQ + KT - 1) // KT
        q0 = i * TQ
    else:
        n_tiles = n_kt
        q0 = q_pos0
    lane = lax.broadcasted_iota(i32, (1, TQ), 1)
    qchunk = (q0 + (lane & (tq_real - 1))) // CHUNK
    n_adm = jnp.minimum((qchunk + 1) * CHUNK, n_keys)

    def fold8(a, op):
        return op(a.reshape(a.shape[0] // SUBLANES, SUBLANES, a.shape[1]), axis=0)

    def score_tile(t, carry):
        kmax8, kmin8, nz8, np8 = carry
        ki_t = ki_ref[t]
        r = jnp.maximum(jnp.dot(ki_t, qiT_ref[...], preferred_element_type=f32), 0.0)
        acc = r[:, :TQ] * wi_ref[0:1, :]
        for h in range(1, N_IDX_HEADS):
            acc = acc + r[:, h * TQ:(h + 1) * TQ] * wi_ref[h:h + 1, :]
        kpos = t * KT + lax.broadcasted_iota(i32, (KT, TQ), 0)
        adm = (kpos // CHUNK) <= qchunk
        acc = jnp.where(acc == 0.0, 0.0, acc)
        key = _sortable(pltpu.bitcast(acc, i32))
        mkey = jnp.where(adm, key, INT_MIN)
        skey_ref[t] = mkey
        kmax8 = jnp.maximum(kmax8, fold8(mkey, jnp.max))
        kmin8 = jnp.minimum(kmin8, fold8(jnp.where(adm, key, ~INT_MIN), jnp.min))
        nz8 = nz8 + fold8(jnp.where(mkey >= 0, 1, 0).astype(i32), jnp.sum)
        np8 = np8 + fold8(jnp.where(mkey >= 1, 1, 0).astype(i32), jnp.sum)
        return kmax8, kmin8, nz8, np8

    zero8 = jnp.zeros((SUBLANES, TQ), i32)
    kmax8, kmin8, nz8, np8 = lax.fori_loop(
        0, n_tiles, score_tile,
        (jnp.full((SUBLANES, TQ), INT_MIN, i32), jnp.full((SUBLANES, TQ), ~INT_MIN, i32), zero8, zero8))
    kmax = jnp.max(kmax8, axis=0, keepdims=True)
    kmin = jnp.min(kmin8, axis=0, keepdims=True)
    n_ge0 = jnp.sum(nz8, axis=0, keepdims=True)
    n_gt0 = jnp.sum(np8, axis=0, keepdims=True)

    def count(pred_fn):
        def body(t, acc8):
            return acc8 + fold8(jnp.where(pred_fn(skey_ref[t]), 1, 0).astype(i32), jnp.sum)
        acc8 = lax.fori_loop(0, n_tiles, body, jnp.zeros((SUBLANES, TQ), i32))
        return jnp.sum(acc8, axis=0, keepdims=True)

    few = n_adm <= topk
    tie0 = jnp.logical_and(n_gt0 < topk, n_ge0 >= topk)
    hit1 = n_gt0 == topk
    above = n_gt0 > topk

    def n_open(done):
        return jnp.max(jnp.where(done == 0, 1.0, 0.0))

    def close_adjacent(lo, hi, a, thr, cge, done):
        adj = jnp.logical_and(lo + 1 == hi, done == 0)
        return jnp.where(adj, lo, thr), jnp.where(adj, a, cge), jnp.where(adj, 1, done)

    def search_step(core, it):
        lo, hi, a, b, thr, cge, done = core
        thr, cge, done = close_adjacent(lo, hi, a, thr, cge, done)
        lo_v = pltpu.bitcast(_sortable(lo), f32)
        hi_v = pltpu.bitcast(_sortable(hi), f32)
        af, bf = a.astype(f32), b.astype(f32)
        lin = (af - topk) / (af - bf)
        la = jnp.log(af + 0.5)
        lg = (la - np.float32(np.log(topk))) / (la - jnp.log(bf + 0.5))
        frac = jnp.where(af > 4.0 * jnp.maximum(bf, 1.0), lg, lin)
        c_int = _sortable(pltpu.bitcast(lo_v + (hi_v - lo_v) * frac, i32))
        c_mid = (lo >> 1) + (hi >> 1) + (lo & hi & 1)
        c = jnp.where((it & 1) == 0, c_int, c_mid)
        c = jnp.minimum(jnp.maximum(c, lo + 1), hi - 1)
        c = jnp.where(done == 1, thr, c)
        cnt = count(lambda tile: tile >= c)
        live = done == 0
        found = jnp.logical_and(live, cnt == topk)
        up = jnp.logical_and(live, cnt > topk)
        dn = jnp.logical_and(live, cnt < topk)
        return (jnp.where(up, c, lo), jnp.where(dn, c, hi), jnp.where(up, cnt, a), jnp.where(dn, cnt, b),
                jnp.where(found, c, thr), jnp.where(found, cnt, cge), jnp.where(found, 1, done))

    def search_cond(st):
        return jnp.logical_and(st[2] > 0, st[1] < FAST_SEARCH_ITERS)

    def search_body(st):
        core, it, _ = st
        core = search_step(search_step(core, it), it + 1)
        return core, it + 2, n_open(core[6])

    zero = jnp.zeros((1, TQ), i32)
    done0 = jnp.where(jnp.logical_or(few, jnp.logical_or(tie0, hit1)), 1, 0)
    thr0 = jnp.where(few, INT_MIN + 1, jnp.where(tie0, 0, 1))
    cge0 = jnp.where(few, 0, jnp.where(tie0, n_ge0, topk))
    core0 = (jnp.where(above, 1, kmin), jnp.where(above, kmax + 1, 0),
             jnp.where(above, n_gt0, n_adm), jnp.where(above, 0, n_ge0), thr0, cge0, done0)
    (lo, hi, a, _, thr, cge, done), _, _ = lax.while_loop(
        search_cond, search_body, (core0, jnp.int32(0), n_open(done0)))
    thr, cge, done = close_adjacent(lo, hi, a, thr, cge, done)
    sel_ref[0:1, :] = thr
    sel_ref[1:2, :] = cge

    @pl.when(n_open(done) > 0)
    def _():
        def bit_body(j, bst):
            t_u, cnt_at = bst
            c_u = t_u | jnp.left_shift(jnp.int32(1), 31 - j)
            cnt = count(lambda tile: tile >= (c_u ^ INT_MIN))
            ok = cnt >= topk
            return jnp.where(ok, c_u, t_u), jnp.where(ok, cnt, cnt_at)

        t_u, cnt_at = lax.fori_loop(0, 32, bit_body, (zero, zero))
        sel_ref[0:1, :] = jnp.where(done == 1, thr, jnp.maximum(t_u ^ INT_MIN, INT_MIN + 1))
        sel_ref[1:2, :] = jnp.where(done == 1, cge, cnt_at)

    thr = sel_ref[0:1, :]
    cge = sel_ref[1:2, :]

    @pl.when(jnp.max(jnp.where(cge > topk, 1.0, 0.0)) > 0)
    def _():
        need = (topk - count(lambda tile: tile > thr)).astype(f32)
        rr = lax.broadcasted_iota(i32, (KT, KT), 0)
        cc = lax.broadcasted_iota(i32, (KT, KT), 1)
        tri = jnp.where(cc <= rr, 1.0, 0.0).astype(bf16)

        def body(t, seen):
            tile = skey_ref[t]
            eq = tile == thr
            eqf = jnp.where(eq, 1.0, 0.0).astype(bf16)
            rank = jnp.dot(tri, eqf, preferred_element_type=f32) + seen
            skey_ref[t] = jnp.where(jnp.logical_and(eq, rank > need), INT_MIN, tile)
            return rank[KT - 1:KT, :]

        lax.fori_loop(0, n_tiles, body, jnp.zeros((1, TQ), f32))

    n_hq = N_HEADS_A * TQ
    g_hq = GQA_GROUP * TQ

    @pl.when(i == 0)
    def _():
        def kn_tile(t, carry):
            kf = k_ref[t].astype(f32)
            sq = kf * kf
            return tuple(
                jnp.maximum(carry[g], jnp.max(jnp.sum(sq[:, g * HEAD_DIM:(g + 1) * HEAD_DIM], axis=1, keepdims=True),
                                              axis=0, keepdims=True))
                for g in range(N_KV_HEADS))

        kn = lax.fori_loop(0, n_kt, kn_tile, tuple(jnp.zeros((1, 1), f32) for _ in range(N_KV_HEADS)))
        kn_ref[0:1, :] = jnp.concatenate([jnp.broadcast_to(kn[g], (1, g_hq)) for g in range(N_KV_HEADS)], axis=1)

    def logits(t):
        bias = jnp.where(skey_ref[t] >= thr, 0.0, -jnp.inf)
        bias = jnp.concatenate([bias] * N_HEADS_A, axis=1)
        return jnp.dot(k_ref[t], qT_ref[...], preferred_element_type=f32) + bias

    def attend(mrow):
        def pv_tile(t, st):
            l8, acc0, acc1 = st
            pr = jnp.exp2(logits(t) - mrow)
            l8 = l8 + fold8(pr, jnp.sum)
            prb = pr.astype(bf16)
            acc0 = acc0 + jnp.dot(vT_ref[t, :HEAD_DIM, :], prb[:, :g_hq], preferred_element_type=f32)
            acc1 = acc1 + jnp.dot(vT_ref[t, HEAD_DIM:, :], prb[:, g_hq:], preferred_element_type=f32)
            return l8, acc0, acc1

        l8, acc0, acc1 = lax.fori_loop(
            0, n_tiles, pv_tile,
            (jnp.zeros((SUBLANES, n_hq), f32), jnp.zeros((HEAD_DIM, g_hq), f32), jnp.zeros((HEAD_DIM, g_hq), f32)))
        lsum = jnp.sum(l8, axis=0, keepdims=True)
        linv = 1.0 / lsum
        oT = [acc0 * linv[:, :g_hq], acc1 * linv[:, g_hq:]]
        for j in range(D_A // LANES):
            g, hh = (2 * j) // GQA_GROUP, (2 * j) % GQA_GROUP
            pair = oT[g][:, hh * TQ:(hh + 2) * TQ]
            stacked = jnp.concatenate([pair[:, :TQ], pair[:, TQ:]], axis=0)
            ab_ref[:, j * LANES:(j + 1) * LANES] = stacked.T.astype(ab_ref.dtype)
        return jnp.min(lsum)

    qf = qT_ref[...].astype(f32)
    lmin = attend(jnp.sqrt(jnp.sum(qf * qf, axis=0, keepdims=True) * kn_ref[0:1, :]) * BOUND_SLACK)

    @pl.when(jnp.logical_not(lmin >= L_TINY))
    def _():
        m8 = lax.fori_loop(0, n_tiles, lambda t, m: jnp.maximum(m, fold8(logits(t), jnp.max)),
                           jnp.full((SUBLANES, n_hq), -jnp.inf, f32))
        attend(jnp.maximum(jnp.max(m8, axis=0, keepdims=True), NEG_BIG))

    lane_c = lax.broadcasted_iota(i32, (GMLP_CHUNK, LANES), 1)
    for p2 in range(N_GROUPS_B // 2):
        xg = gvn_ref[:, p2 * LANES:(p2 + 1) * LANES].astype(bf16)
        both = jnp.dot(wm_ref[p2], xg, preferred_element_type=f32)
        mix = jnp.where(lane_c < GROUP_DIM_B, both[:GMLP_CHUNK], both[GMLP_CHUNK:])
        mix = mix + bias_ref[:, p2 * LANES:(p2 + 1) * LANES]
        gu = gu_ref[:, p2 * LANES:(p2 + 1) * LANES].astype(f32)
        ab_ref[:, D_A + p2 * LANES:D_A + (p2 + 1) * LANES] = (gu * mix).astype(ab_ref.dtype)


def _dsa_call(qT, qiT, wi, k, vT, ki, gvn, gu, wm, bias, *, causal, q_pos0, tq_real, n_keys, topk):
    nbatch, nblk = qT.shape[:2]
    n_kt = k.shape[1]
    kern = functools.partial(_dsa_kernel, causal=causal, q_pos0=q_pos0, tq_real=tq_real,
                             n_keys=n_keys, n_kt=n_kt, topk=topk)
    blk4 = lambda b, i: (b, i, 0, 0)
    per_b = lambda b, i: (b, 0, 0, 0)
    return pl.pallas_call(
        kern,
        grid=(nbatch, nblk),
        in_specs=[
            pl.BlockSpec((None, None, D_KV, N_HEADS_A * TQ), blk4),
            pl.BlockSpec((None, None, IDX_DIM, N_IDX_HEADS * TQ), blk4),
            pl.BlockSpec((None, None, N_IDX_HEADS, TQ), blk4),
            pl.BlockSpec((None, n_kt, KT, D_KV), per_b),
            pl.BlockSpec((None, n_kt, D_KV, KT), per_b),
            pl.BlockSpec((None, n_kt, KT, IDX_DIM), per_b),
            pl.BlockSpec((None, None, TQ, D_B), blk4),
            pl.BlockSpec((None, None, TQ, D_B), blk4),
            pl.BlockSpec((N_GROUPS_B // 2, 2 * GMLP_CHUNK, GMLP_CHUNK), lambda b, i: (0, 0, 0)),
            pl.BlockSpec((GMLP_CHUNK, D_B), lambda b, i: (0, 0)),
        ],
        out_specs=pl.BlockSpec((None, None, TQ, D_A + D_B), blk4),
        out_shape=jax.ShapeDtypeStruct((nbatch, nblk, TQ, D_A + D_B), jnp.bfloat16),
        scratch_shapes=[pltpu.VMEM((n_kt, KT, TQ), jnp.int32),
                        pltpu.VMEM((SUBLANES, TQ), jnp.int32),
                        pltpu.VMEM((SUBLANES, N_HEADS_A * TQ), jnp.float32)],
        compiler_params=pltpu.CompilerParams(
            dimension_semantics=("arbitrary", "arbitrary"), vmem_limit_bytes=VMEM_LIMIT),
        name="dsa_prompt" if causal else "dsa_sample",
    )(qT, qiT, wi, k, vT, ki, gvn, gu, wm, bias)


def _ffn_kernel(x_ref, ab_ref, wo_ref, g_ref, w1_ref, w2_ref, gf_ref, o_ref, *, final):
    f32 = jnp.float32
    x = x_ref[...] + jnp.dot(ab_ref[...], wo_ref[...], preferred_element_type=f32)
    ms = jnp.mean(x * x, axis=-1, keepdims=True)
    xn = (x * lax.rsqrt(ms + EPS) * g_ref[...]).astype(jnp.bfloat16)
    h = jnp.maximum(jnp.dot(xn, w1_ref[...], preferred_element_type=f32), 0.0)
    acc = x + jnp.dot((h * h).astype(jnp.bfloat16), w2_ref[...], preferred_element_type=f32)
    if final:
        ms2 = jnp.mean(acc * acc, axis=-1, keepdims=True)
        acc = acc * lax.rsqrt(ms2 + EPS) * gf_ref[...]
    o_ref[...] = acc


def _ffn_call(x, ab, wo, g, w1, w2, gf, *, final):
    rows = x.shape[0]
    row_map = lambda i: (i, 0)
    const_map = lambda i: (0, 0)
    once = pl.Buffered(1)
    return pl.pallas_call(
        functools.partial(_ffn_kernel, final=final),
        grid=(rows // TM,),
        in_specs=[
            pl.BlockSpec((TM, D_MODEL), row_map),
            pl.BlockSpec((TM, D_A + D_B), row_map),
            pl.BlockSpec((D_A + D_B, D_MODEL), const_map, pipeline_mode=once),
            pl.BlockSpec((1, D_MODEL), const_map),
            pl.BlockSpec((D_MODEL, D_FF), const_map, pipeline_mode=once),
            pl.BlockSpec((D_FF, D_MODEL), const_map, pipeline_mode=once),
            pl.BlockSpec((1, D_MODEL), const_map),
        ],
        out_specs=pl.BlockSpec((TM, D_MODEL), row_map),
        out_shape=jax.ShapeDtypeStruct((rows, D_MODEL), jnp.float32),
        compiler_params=pltpu.CompilerParams(
            dimension_semantics=("arbitrary",), vmem_limit_bytes=VMEM_LIMIT),
        name="ffn",
    )(x, ab, wo, g, w1, w2, gf)


def _rope_tables(pos):
    posf = pos.astype(jnp.float32)[:, None]
    lane = jnp.arange(LANES)

    def one(width, rot, limit):
        half = rot // 2
        d = lane % width
        inv = ROPE_THETA ** (-jnp.arange(half, dtype=jnp.float32) * (2.0 / rot))
        ang = posf * inv[None, :]
        cos_l = jnp.take(jnp.cos(ang), d % half, axis=1)
        sin_l = jnp.take(jnp.sin(ang), d % half, axis=1)
        live = lane < limit
        lo = jnp.logical_and(d < half, live)[None, :]
        hi = jnp.logical_and(jnp.logical_and(d >= half, d < rot), live)[None, :]
        c = jnp.where(jnp.logical_or(lo, hi), cos_l, 1.0)
        return [c, jnp.where(lo, -sin_l, 0.0), jnp.where(hi, sin_l, 0.0)]

    tabs = one(HEAD_DIM, ROT_DIM, LANES) + one(IDX_DIM, IDX_ROT, LANES) + one(IDX_DIM, IDX_ROT, IDX_DIM)
    return jnp.concatenate(tabs, axis=1)


def kernel(x_prompt, x_sample, cache_k, cache_v, cache_kidx, norm_mix, w_in, gate_norm, w_spatial,
           b_spatial, w_out, norm_ffn, w_ff1, w_ff2, norm_final):
    f32, bf16 = jnp.float32, jnp.bfloat16
    depth = w_in.shape[0]
    nb_p, t_p = x_prompt.shape[:2]
    nb_s, t_s = x_sample.shape[:2]
    past = cache_k.shape[2]
    rows_p, rows_s = nb_p * t_p, nb_s * TQ
    assert t_p % TM == 0 and rows_s % TM == 0 and t_p % KT == 0
    assert t_s * 2 == TQ and t_s == CHUNK and past % CHUNK == 0
    nblk_p = t_p // TQ
    nkt_p = t_p // KT
    topk_p = min(TOPK_MAX, t_p // 4)
    topk_s = min(TOPK_MAX, (past + t_s) // 4)
    s_all = past + t_s
    s_pad = -(-s_all // KT) * KT
    nkt_s = s_pad // KT

    n_head_cols = _U0 - 88
    w_in_p = jnp.concatenate(
        [w_in[:, :, :n_head_cols], jnp.zeros((depth, D_MODEL, _U0 - n_head_cols), w_in.dtype),
         w_in[:, :, n_head_cols:]], axis=2).astype(bf16)
    w_out_b = w_out.astype(bf16)
    w1_b = w_ff1.astype(bf16)
    w2_b = w_ff2.astype(bf16)
    cm = (jnp.arange(GMLP_CHUNK)[None, :] // CHUNK <= jnp.arange(GMLP_CHUNK)[:, None] // CHUNK)
    wm_p = jnp.where(cm[None, None], w_spatial, 0.0).astype(bf16)
    small = (jnp.arange(GMLP_CHUNK) < t_s)
    wm_s = jnp.where(jnp.logical_and(small[:, None], small[None, :])[None, None], w_spatial, 0.0).astype(bf16)
    wm_p = wm_p.reshape(depth, N_GROUPS_B // 2, 2 * GMLP_CHUNK, GMLP_CHUNK)
    wm_s = wm_s.reshape(depth, N_GROUPS_B // 2, 2 * GMLP_CHUNK, GMLP_CHUNK)
    bias_sp = jnp.repeat(jnp.swapaxes(b_spatial, 1, 2), GROUP_DIM_B, axis=2)

    pos_s = past + jnp.arange(t_s, dtype=jnp.int32)
    tab_p = _rope_tables(jnp.arange(t_p, dtype=jnp.int32))
    tab_s = _rope_tables(jnp.tile(jnp.concatenate([pos_s, pos_s]), TM // TQ))

    cache_kb = cache_k.astype(bf16).reshape(depth, nb_s, past, D_KV)
    cache_vb = cache_v.astype(bf16).reshape(depth, nb_s, past, D_KV)
    cache_kib = cache_kidx.astype(bf16)
    pad_rows = jnp.zeros((nb_s, s_pad - s_all, D_KV), bf16)

    xp = x_prompt.reshape(rows_p, D_MODEL)
    xs = jnp.concatenate([x_sample, x_sample], axis=1).reshape(rows_s, D_MODEL)

    def first_half(a):
        return a.reshape(a.shape[:-2] + (nb_s, TQ, a.shape[-1]))[..., :t_s, :]

    stk_p = stk_s = None
    gvs_l = []
    for l in range(depth):
        g_mix, g_gate, g_ffn = norm_mix[l][None, :], gate_norm[l][None, :], norm_ffn[l][None, :]
        final = l == depth - 1

        qT, qiT, wiT, k_st, v_st, ki_st, kb, vT, kib, gu, gvn = _proj_call(
            xp, g_mix, w_in_p[l], tab_p, g_gate, l, depth, stk_p, bf16)
        stk_p = (k_st, v_st, ki_st)
        ab = _dsa_call(
            qT.reshape((nb_p, nblk_p) + qT.shape[1:]), qiT.reshape((nb_p, nblk_p) + qiT.shape[1:]),
            wiT.reshape((nb_p, nblk_p) + wiT.shape[1:]),
            kb.reshape(nb_p, nkt_p, KT, D_KV), vT.reshape(nb_p, nkt_p, D_KV, KT),
            kib.reshape(nb_p, nkt_p, KT, IDX_DIM),
            gvn.reshape(nb_p, nblk_p, TQ, D_B), gu.reshape(nb_p, nblk_p, TQ, D_B),
            wm_p[l], bias_sp[l], causal=True, q_pos0=0, tq_real=TQ, n_keys=t_p, topk=topk_p)
        xp = _ffn_call(xp, ab.reshape(rows_p, D_A + D_B), w_out_b[l], g_ffn, w1_b[l], w2_b[l],
                       norm_final[None, :], final=final)

        qT, qiT, wiT, k_st, v_st, ki_st, kb, vT, kib, gu, gvn = _proj_call(
            xs, g_mix, w_in_p[l], tab_s, g_gate, l, depth, stk_s, f32)
        stk_s = (k_st, v_st, ki_st)
        k_all = jnp.concatenate([cache_kb[l], first_half(kb), pad_rows], axis=1)
        v_all = jnp.concatenate([cache_vb[l], first_half(v_st[l].reshape(rows_s, D_KV)).astype(bf16), pad_rows], axis=1)
        ki_all = jnp.concatenate([cache_kib[l], first_half(kib), pad_rows[:, :, :IDX_DIM]], axis=1)
        ab = _dsa_call(
            qT.reshape((nb_s, 1) + qT.shape[1:]), qiT.reshape((nb_s, 1) + qiT.shape[1:]),
            wiT.reshape((nb_s, 1) + wiT.shape[1:]),
            k_all.reshape(nb_s, nkt_s, KT, D_KV),
            jnp.swapaxes(v_all.reshape(nb_s, nkt_s, KT, D_KV), 2, 3),
            ki_all.reshape(nb_s, nkt_s, KT, IDX_DIM),
            gvn.reshape(nb_s, 1, TQ, D_B), gu.reshape(nb_s, 1, TQ, D_B),
            wm_s[l], bias_sp[l], causal=False, q_pos0=past, tq_real=t_s, n_keys=s_all, topk=topk_s)
        xs = _ffn_call(xs, ab.reshape(rows_s, D_A + D_B), w_out_b[l], g_ffn, w1_b[l], w2_b[l],
                       norm_final[None, :], final=final)
        gvs_l.append(first_half(gvn))

    k_p, v_p, ki_p = stk_p
    k_s, v_s, ki_s = (first_half(a.reshape(depth, rows_s, -1)) for a in stk_s)
    return (xp.reshape(nb_p, t_p, D_MODEL), first_half(xs),
            k_p.reshape(depth, nb_p, t_p, N_KV_HEADS, HEAD_DIM),
            v_p.reshape(depth, nb_p, t_p, N_KV_HEADS, HEAD_DIM),
            ki_p.reshape(depth, nb_p, t_p, IDX_DIM),
            k_s.reshape(depth, nb_s, t_s, N_KV_HEADS, HEAD_DIM),
            v_s.reshape(depth, nb_s, t_s, N_KV_HEADS, HEAD_DIM),
            ki_s, jnp.stack(gvs_l))
```

```python
import functools

import numpy as np
import jax
import jax.numpy as jnp
from jax import lax
from jax.experimental import pallas as pl
from jax.experimental.pallas import tpu as pltpu

D_MODEL = 1024
CHUNK = 64
D_A = 512
D_B = 512
HEAD_DIM = 64
N_HEADS_A = 8
N_KV_HEADS = 2
GQA_GROUP = 4
D_KV = 128
ROT_DIM = 16
N_IDX_HEADS = 8
IDX_DIM = 32
IDX_ROT = 8
TOPK_MAX = 256
N_GROUPS_B = 8
GROUP_DIM_B = 64
GMLP_CHUNK = 128
D_FF = 4096
ROPE_THETA = 500000.0
EPS = 1e-6

LANES = 128
SUBLANES = 8
N_IN_PAD = 2176
TM = 512
TQ = 128
KT = 512
VMEM_LIMIT = 56 * 1024 * 1024

INT_MIN = np.int32(-2147483648)
MANT_MASK = np.int32(0x7FFFFFFF)
FAST_SEARCH_ITERS = 40
BOUND_SLACK = 1.02
L_TINY = 1e-30
NEG_BIG = -1e30
LOG2E = 1.4426950408889634

_Q0, _K0, _V0, _QI0, _KW0, _U0, _GV0 = 0, 512, 640, 768, 1024, 1152, 1664


def _gelu(x):
    return x * (0.5 * (1.0 + jnp.tanh(0.7978845608028654 * (x + 0.044715 * (x * x * x)))))


def _rope128(x, c, s_lo, s_hi, half):
    return x * c + pltpu.roll(x, LANES - half, 1) * s_lo + pltpu.roll(x, half, 1) * s_hi


def _sortable(bits):
    return bits ^ ((bits >> 31) & MANT_MASK)


def _proj_kernel(*refs, n_alias):
    (x_ref, g_ref, w_ref, tab_ref, gn_ref) = refs[:5]
    (qT_ref, qiT_ref, wi_ref, k_ref, v_ref, ki_ref, kb_ref, vT_ref, kib_ref, gu_ref, gvn_ref) = refs[5 + n_alias:]
    bf16 = jnp.bfloat16
    x = x_ref[...]
    ms = jnp.mean(x * x, axis=-1, keepdims=True)
    xn = (x * lax.rsqrt(ms + EPS) * g_ref[...]).astype(bf16)
    z = jnp.dot(xn, w_ref[...], preferred_element_type=jnp.float32)

    def tab(j):
        return tab_ref[:, j * LANES:(j + 1) * LANES]

    cq, sq_lo, sq_hi = tab(0), tab(1), tab(2)
    ci, si_lo, si_hi = tab(3), tab(4), tab(5)
    ck, sk_lo, sk_hi = tab(6), tab(7), tab(8)
    n_blk = TM // TQ

    def blocks_T(slab):
        return [slab[b * TQ:(b + 1) * TQ, :].T for b in range(n_blk)]

    qT_ref[...] = jnp.zeros(qT_ref.shape, qT_ref.dtype)
    for j in range(D_A // LANES):
        zq = _rope128(z[:, _Q0 + j * LANES:_Q0 + (j + 1) * LANES], cq, sq_lo, sq_hi, ROT_DIM // 2)
        zq = zq * (HEAD_DIM ** -0.5 * LOG2E)
        for b, t in enumerate(blocks_T(zq)):
            for r in range(LANES // HEAD_DIM):
                h = j * (LANES // HEAD_DIM) + r
                g = h // GQA_GROUP
                qT_ref[b, g * HEAD_DIM:(g + 1) * HEAD_DIM, h * TQ:(h + 1) * TQ] = (
                    t[r * HEAD_DIM:(r + 1) * HEAD_DIM, :].astype(bf16))

    kk = _rope128(z[:, _K0:_K0 + LANES], cq, sq_lo, sq_hi, ROT_DIM // 2)
    for g in range(N_KV_HEADS):
        k_ref[:, g, :] = kk[:, g * HEAD_DIM:(g + 1) * HEAD_DIM]
    kb_ref[...] = kk.astype(bf16)
    vv = z[:, _V0:_V0 + LANES]
    for g in range(N_KV_HEADS):
        v_ref[:, g, :] = vv[:, g * HEAD_DIM:(g + 1) * HEAD_DIM]
    for b, t in enumerate(blocks_T(vv)):
        kt, off = (b * TQ) // KT, (b * TQ) % KT
        vT_ref[kt, :, off:off + TQ] = t.astype(bf16)

    for j in range(N_IDX_HEADS * IDX_DIM // LANES):
        zi = _rope128(z[:, _QI0 + j * LANES:_QI0 + (j + 1) * LANES], ci, si_lo, si_hi, IDX_ROT // 2)
        for b, t in enumerate(blocks_T(zi)):
            for r in range(LANES // IDX_DIM):
                h = j * (LANES // IDX_DIM) + r
                qiT_ref[b, :, h * TQ:(h + 1) * TQ] = t[r * IDX_DIM:(r + 1) * IDX_DIM, :].astype(bf16)

    kw = _rope128(z[:, _KW0:_KW0 + LANES], ck, sk_lo, sk_hi, IDX_ROT // 2)
    ki_ref[...] = kw[:, :IDX_DIM]
    kib_ref[...] = kw[:, :IDX_DIM].astype(bf16)
    for b, t in enumerate(blocks_T(kw)):
        wi_ref[b] = t[IDX_DIM:IDX_DIM + N_IDX_HEADS, :] * (IDX_DIM ** -0.5 * N_IDX_HEADS ** -0.5)

    gu_ref[...] = _gelu(z[:, _U0:_U0 + D_B]).astype(gu_ref.dtype)
    gg = _gelu(z[:, _GV0:_GV0 + D_B])
    gms = jnp.mean(gg * gg, axis=-1, keepdims=True)
    gvn_ref[...] = (gg * lax.rsqrt(gms + EPS) * gn_ref[...]).astype(gvn_ref.dtype)


def _proj_call(x, g, w, tab, gn, layer, depth, stacks, gvn_dtype):
    rows = x.shape[0]
    n_tiles = rows // TM
    n_pos_tiles = tab.shape[0] // TM
    n_blk = TM // TQ
    f32, bf16 = jnp.float32, jnp.bfloat16
    n_alias = 0 if stacks is None else 3

    row2 = lambda i: (i, 0)
    const2 = lambda i: (0, 0)
    lead3 = lambda i: (i, 0, 0)
    lay3 = lambda i: (layer, i, 0)
    lay4 = lambda i: (layer, i, 0, 0)
    outs = [
        ((rows // TQ, D_KV, N_HEADS_A * TQ), bf16, (n_blk, D_KV, N_HEADS_A * TQ), lead3),
        ((rows // TQ, IDX_DIM, N_IDX_HEADS * TQ), bf16, (n_blk, IDX_DIM, N_IDX_HEADS * TQ), lead3),
        ((rows // TQ, N_IDX_HEADS, TQ), f32, (n_blk, N_IDX_HEADS, TQ), lead3),
        ((depth, rows, N_KV_HEADS, HEAD_DIM), f32, (None, TM, N_KV_HEADS, HEAD_DIM), lay4),
        ((depth, rows, N_KV_HEADS, HEAD_DIM), f32, (None, TM, N_KV_HEADS, HEAD_DIM), lay4),
        ((depth, rows, IDX_DIM), f32, (None, TM, IDX_DIM), lay3),
        ((rows, D_KV), bf16, (TM, D_KV), row2),
        ((rows // KT, D_KV, KT), bf16, (TM // KT, D_KV, KT), lead3),
        ((rows, IDX_DIM), bf16, (TM, IDX_DIM), row2),
        ((rows, D_B), bf16, (TM, D_B), row2),
        ((rows, D_B), gvn_dtype, (TM, D_B), row2),
    ]
    in_specs = [
        pl.BlockSpec((TM, D_MODEL), row2),
        pl.BlockSpec((1, D_MODEL), const2),
        pl.BlockSpec((D_MODEL, N_IN_PAD), const2),
        pl.BlockSpec((TM, 9 * LANES), lambda i: (i % n_pos_tiles, 0)),
        pl.BlockSpec((1, D_B), const2),
    ] + [pl.BlockSpec(memory_space=pl.ANY)] * n_alias
    args = (x, g, w, tab, gn) + (() if stacks is None else tuple(stacks))
    return pl.pallas_call(
        functools.partial(_proj_kernel, n_alias=n_alias),
        grid=(n_tiles,),
        in_specs=in_specs,
        out_specs=tuple(pl.BlockSpec(blk, imap) for _, _, blk, imap in outs),
        out_shape=tuple(jax.ShapeDtypeStruct(shape, dt) for shape, dt, _, _ in outs),
        input_output_aliases={5 + j: 3 + j for j in range(n_alias)},
        compiler_params=pltpu.CompilerParams(
            dimension_semantics=("arbitrary",), vmem_limit_bytes=VMEM_LIMIT),
        name="proj",
    )(*args)


def _dsa_kernel(qT_ref, qiT_ref, wi_ref, k_ref, vT_ref, ki_ref, gvn_ref, gu_ref, wm_ref, bias_ref,
                ab_ref, skey_ref, sel_ref, kn_ref, *, causal, q_pos0, tq_real, n_keys, n_kt, topk):
    f32, bf16, i32 = jnp.float32, jnp.bfloat16, jnp.int32
    i = pl.program_id(1)
    if causal:
        n_tiles = ((i + 1) * TQ + KT - 1) // KT
        q0 = i * TQ
    else:
        n_tiles = n_kt
        q0 = q_pos0
    lane = lax.broadcasted_iota(i32, (1, TQ), 1)
    qchunk = (q0 + (lane & (tq_real - 1))) // CHUNK
    n_adm = jnp.minimum((qchunk + 1) * CHUNK, n_keys)

    def fold8(a, op):
        return op(a.reshape(a.shape[0] // SUBLANES, SUBLANES, a.shape[1]), axis=0)

    def score_tile(t, carry, partial):
        kmax8, kmin8, nz8, np8 = carry
        ki_t = ki_ref[t]
        r = jnp.maximum(jnp.dot(ki_t, qiT_ref[...], preferred_element_type=f32), 0.0)
        acc = r[:, :TQ] * wi_ref[0:1, :]
        for h in range(1, N_IDX_HEADS):
            acc = acc + r[:, h * TQ:(h + 1) * TQ] * wi_ref[h:h + 1, :]
        acc = jnp.where(acc == 0.0, 0.0, acc)
        key = _sortable(pltpu.bitcast(acc, i32))
        if partial:
            kpos = t * KT + lax.broadcasted_iota(i32, (KT, TQ), 0)
            adm = (kpos // CHUNK) <= qchunk
            mkey = jnp.where(adm, key, INT_MIN)
            lowkey = jnp.where(adm, key, ~INT_MIN)
        else:
            mkey = lowkey = key
        skey_ref[t] = mkey
        kmax8 = jnp.maximum(kmax8, fold8(mkey, jnp.max))
        kmin8 = jnp.minimum(kmin8, fold8(lowkey, jnp.min))
        nz8 = nz8 + fold8(jnp.where(mkey >= 0, 1, 0).astype(i32), jnp.sum)
        np8 = np8 + fold8(jnp.where(mkey >= 1, 1, 0).astype(i32), jnp.sum)
        return kmax8, kmin8, nz8, np8

    n_whole = (i * TQ) // KT if causal else n_keys // KT
    zero8 = jnp.zeros((SUBLANES, TQ), i32)
    carry = (jnp.full((SUBLANES, TQ), INT_MIN, i32), jnp.full((SUBLANES, TQ), ~INT_MIN, i32), zero8, zero8)
    carry = lax.fori_loop(0, n_whole, functools.partial(score_tile, partial=False), carry)
    kmax8, kmin8, nz8, np8 = lax.fori_loop(n_whole, n_tiles, functools.partial(score_tile, partial=True), carry)
    kmax = jnp.max(kmax8, axis=0, keepdims=True)
    kmin = jnp.min(kmin8, axis=0, keepdims=True)
    n_ge0 = jnp.sum(nz8, axis=0, keepdims=True)
    n_gt0 = jnp.sum(np8, axis=0, keepdims=True)

    def count(pred_fn):
        def body(t, acc8):
            return acc8 + fold8(jnp.where(pred_fn(skey_ref[t]), 1, 0).astype(i32), jnp.sum)
        acc8 = lax.fori_loop(0, n_tiles, body, jnp.zeros((SUBLANES, TQ), i32))
        return jnp.sum(acc8, axis=0, keepdims=True)

    few = n_adm <= topk
    tie0 = jnp.logical_and(n_gt0 < topk, n_ge0 >= topk)
    hit1 = n_gt0 == topk
    above = n_gt0 > topk

    def n_open(done):
        return jnp.max(jnp.where(done == 0, 1.0, 0.0))

    def close_adjacent(lo, hi, a, thr, cge, done):
        adj = jnp.logical_and(lo + 1 == hi, done == 0)
        return jnp.where(adj, lo, thr), jnp.where(adj, a, cge), jnp.where(adj, 1, done)

    def search_step(core, it):
        lo, hi, a, b, thr, cge, done = core
        thr, cge, done = close_adjacent(lo, hi, a, thr, cge, done)
        lo_v = pltpu.bitcast(_sortable(lo), f32)
        hi_v = pltpu.bitcast(_sortable(hi), f32)
        af, bf = a.astype(f32), b.astype(f32)
        lin = (af - topk) / (af - bf)
        la = jnp.log(af + 0.5)
        lg = (la - np.float32(np.log(topk))) / (la - jnp.log(bf + 0.5))
        frac = jnp.where(af > 4.0 * jnp.maximum(bf, 1.0), lg, lin)
        c_int = _sortable(pltpu.bitcast(lo_v + (hi_v - lo_v) * frac, i32))
        c_mid = (lo >> 1) + (hi >> 1) + (lo & hi & 1)
        c = jnp.where((it & 1) == 0, c_int, c_mid)
        c = jnp.minimum(jnp.maximum(c, lo + 1), hi - 1)
        c = jnp.where(done == 1, thr, c)
        cnt = count(lambda tile: tile >= c)
        live = done == 0
        found = jnp.logical_and(live, cnt == topk)
        up = jnp.logical_and(live, cnt > topk)
        dn = jnp.logical_and(live, cnt < topk)
        return (jnp.where(up, c, lo), jnp.where(dn, c, hi), jnp.where(up, cnt, a), jnp.where(dn, cnt, b),
                jnp.where(found, c, thr), jnp.where(found, cnt, cge), jnp.where(found, 1, done))

    def search_cond(st):
        return jnp.logical_and(st[2] > 0, st[1] < FAST_SEARCH_ITERS)

    def search_body(st):
        core, it, _ = st
        core = search_step(search_step(core, it), it + 1)
        return core, it + 2, n_open(core[6])

    zero = jnp.zeros((1, TQ), i32)
    done0 = jnp.where(jnp.logical_or(few, jnp.logical_or(tie0, hit1)), 1, 0)
    thr0 = jnp.where(few, INT_MIN + 1, jnp.where(tie0, 0, 1))
    cge0 = jnp.where(few, 0, jnp.where(tie0, n_ge0, topk))
    core0 = (jnp.where(above, 1, kmin), jnp.where(above, kmax + 1, 0),
             jnp.where(above, n_gt0, n_adm), jnp.where(above, 0, n_ge0), thr0, cge0, done0)
    (lo, hi, a, _, thr, cge, done), _, _ = lax.while_loop(
        search_cond, search_body, (core0, jnp.int32(0), n_open(done0)))
    thr, cge, done = close_adjacent(lo, hi, a, thr, cge, done)
    sel_ref[0:1, :] = thr
    sel_ref[1:2, :] = cge

    @pl.when(n_open(done) > 0)
    def _():
        def bit_body(j, bst):
            t_u, cnt_at = bst
            c_u = t_u | jnp.left_shift(jnp.int32(1), 31 - j)
            cnt = count(lambda tile: tile >= (c_u ^ INT_MIN))
            ok = cnt >= topk
            return jnp.where(ok, c_u, t_u), jnp.where(ok, cnt, cnt_at)

        t_u, cnt_at = lax.fori_loop(0, 32, bit_body, (zero, zero))
        sel_ref[0:1, :] = jnp.where(done == 1, thr, jnp.maximum(t_u ^ INT_MIN, INT_MIN + 1))
        sel_ref[1:2, :] = jnp.where(done == 1, cge, cnt_at)

    thr = sel_ref[0:1, :]
    cge = sel_ref[1:2, :]

    @pl.when(jnp.max(jnp.where(cge > topk, 1.0, 0.0)) > 0)
    def _():
        need = (topk - count(lambda tile: tile > thr)).astype(f32)
        rr = lax.broadcasted_iota(i32, (KT, KT), 0)
        cc = lax.broadcasted_iota(i32, (KT, KT), 1)
        tri = jnp.where(cc <= rr, 1.0, 0.0).astype(bf16)

        def body(t, seen):
            tile = skey_ref[t]
            eq = tile == thr
            eqf = jnp.where(eq, 1.0, 0.0).astype(bf16)
            rank = jnp.dot(tri, eqf, preferred_element_type=f32) + seen
            skey_ref[t] = jnp.where(jnp.logical_and(eq, rank > need), INT_MIN, tile)
            return rank[KT - 1:KT, :]

        lax.fori_loop(0, n_tiles, body, jnp.zeros((1, TQ), f32))

    n_hq = N_HEADS_A * TQ
    g_hq = GQA_GROUP * TQ

    @pl.when(i == 0)
    def _():
        def kn_tile(t, carry):
            kf = k_ref[t].astype(f32)
            sq = kf * kf
            return tuple(
                jnp.maximum(carry[g], jnp.max(jnp.sum(sq[:, g * HEAD_DIM:(g + 1) * HEAD_DIM], axis=1, keepdims=True),
                                              axis=0, keepdims=True))
                for g in range(N_KV_HEADS))

        kn = lax.fori_loop(0, n_kt, kn_tile, tuple(jnp.zeros((1, 1), f32) for _ in range(N_KV_HEADS)))
        kn_ref[0:1, :] = jnp.concatenate([jnp.broadcast_to(kn[g], (1, g_hq)) for g in range(N_KV_HEADS)], axis=1)

    def logits(t):
        bias = jnp.where(skey_ref[t] >= thr, 0.0, -jnp.inf)
        bias = jnp.concatenate([bias] * N_HEADS_A, axis=1)
        return jnp.dot(k_ref[t], qT_ref[...], preferred_element_type=f32) + bias

    def attend(mrow):
        ones_rows = jnp.ones((SUBLANES, KT), bf16)

        def pv_tile(t, st):
            prb = jnp.exp2(logits(t) - mrow).astype(bf16)
            out = []
            for g in range(N_KV_HEADS):
                lhs = jnp.concatenate([vT_ref[t, g * HEAD_DIM:(g + 1) * HEAD_DIM, :], ones_rows], axis=0)
                out.append(st[g] + jnp.dot(lhs, prb[:, g * g_hq:(g + 1) * g_hq], preferred_element_type=f32))
            return tuple(out)

        acc = lax.fori_loop(0, n_tiles, pv_tile,
                            tuple(jnp.zeros((HEAD_DIM + SUBLANES, g_hq), f32) for _ in range(N_KV_HEADS)))
        acc0, acc1 = acc[0][:HEAD_DIM], acc[1][:HEAD_DIM]
        l8 = jnp.concatenate([acc[0][HEAD_DIM:], acc[1][HEAD_DIM:]], axis=1)
        lsum = l8[0:1, :]
        linv = 1.0 / lsum
        oT = [acc0 * linv[:, :g_hq], acc1 * linv[:, g_hq:]]
        for j in range(D_A // LANES):
            g, hh = (2 * j) // GQA_GROUP, (2 * j) % GQA_GROUP
            pair = oT[g][:, hh * TQ:(hh + 2) * TQ]
            stacked = jnp.concatenate([pair[:, :TQ], pair[:, TQ:]], axis=0)
            ab_ref[:, j * LANES:(j + 1) * LANES] = stacked.T.astype(ab_ref.dtype)
        return jnp.min(lsum)

    qf = qT_ref[...].astype(f32)
    lmin = attend(jnp.sqrt(jnp.sum(qf * qf, axis=0, keepdims=True) * kn_ref[0:1, :]) * BOUND_SLACK)

    @pl.when(jnp.logical_not(lmin >= L_TINY))
    def _():
        m8 = lax.fori_loop(0, n_tiles, lambda t, m: jnp.maximum(m, fold8(logits(t), jnp.max)),
                           jnp.full((SUBLANES, n_hq), -jnp.inf, f32))
        attend(jnp.maximum(jnp.max(m8, axis=0, keepdims=True), NEG_BIG))

    lane_c = lax.broadcasted_iota(i32, (GMLP_CHUNK, LANES), 1)
    for p2 in range(N_GROUPS_B // 2):
        xg = gvn_ref[:, p2 * LANES:(p2 + 1) * LANES].astype(bf16)
        both = jnp.dot(wm_ref[p2], xg, preferred_element_type=f32)
        mix = jnp.where(lane_c < GROUP_DIM_B, both[:GMLP_CHUNK], both[GMLP_CHUNK:])
        mix = mix + bias_ref[:, p2 * LANES:(p2 + 1) * LANES]
        gu = gu_ref[:, p2 * LANES:(p2 + 1) * LANES].astype(f32)
        ab_ref[:, D_A + p2 * LANES:D_A + (p2 + 1) * LANES] = (gu * mix).astype(ab_ref.dtype)


def _dsa_call(qT, qiT, wi, k, vT, ki, gvn, gu, wm, bias, *, causal, q_pos0, tq_real, n_keys, topk):
    nbatch, nblk = qT.shape[:2]
    n_kt = k.shape[1]
    kern = functools.partial(_dsa_kernel, causal=causal, q_pos0=q_pos0, tq_real=tq_real,
                             n_keys=n_keys, n_kt=n_kt, topk=topk)
    blk4 = lambda b, i: (b, i, 0, 0)
    per_b = lambda b, i: (b, 0, 0, 0)
    return pl.pallas_call(
        kern,
        grid=(nbatch, nblk),
        in_specs=[
            pl.BlockSpec((None, None, D_KV, N_HEADS_A * TQ), blk4),
            pl.BlockSpec((None, None, IDX_DIM, N_IDX_HEADS * TQ), blk4),
            pl.BlockSpec((None, None, N_IDX_HEADS, TQ), blk4),
            pl.BlockSpec((None, n_kt, KT, D_KV), per_b),
            pl.BlockSpec((None, n_kt, D_KV, KT), per_b),
            pl.BlockSpec((None, n_kt, KT, IDX_DIM), per_b),
            pl.BlockSpec((None, None, TQ, D_B), blk4),
            pl.BlockSpec((None, None, TQ, D_B), blk4),
            pl.BlockSpec((N_GROUPS_B // 2, 2 * GMLP_CHUNK, GMLP_CHUNK), lambda b, i: (0, 0, 0)),
            pl.BlockSpec((GMLP_CHUNK, D_B), lambda b, i: (0, 0)),
        ],
        out_specs=pl.BlockSpec((None, None, TQ, D_A + D_B), blk4),
        out_shape=jax.ShapeDtypeStruct((nbatch, nblk, TQ, D_A + D_B), jnp.bfloat16),
        scratch_shapes=[pltpu.VMEM((n_kt, KT, TQ), jnp.int32),
                        pltpu.VMEM((SUBLANES, TQ), jnp.int32),
                        pltpu.VMEM((SUBLANES, N_HEADS_A * TQ), jnp.float32)],
        compiler_params=pltpu.CompilerParams(
            dimension_semantics=("arbitrary", "arbitrary"), vmem_limit_bytes=VMEM_LIMIT),
        name="dsa_prompt" if causal else "dsa_sample",
    )(qT, qiT, wi, k, vT, ki, gvn, gu, wm, bias)


def _ffn_kernel(x_ref, ab_ref, wo_ref, g_ref, w1_ref, w2_ref, gf_ref, o_ref, *, final):
    f32 = jnp.float32
    x = x_ref[...] + jnp.dot(ab_ref[...], wo_ref[...], preferred_element_type=f32)
    ms = jnp.mean(x * x, axis=-1, keepdims=True)
    xn = (x * lax.rsqrt(ms + EPS) * g_ref[...]).astype(jnp.bfloat16)
    h = jnp.maximum(jnp.dot(xn, w1_ref[...], preferred_element_type=f32), 0.0)
    acc = x + jnp.dot((h * h).astype(jnp.bfloat16), w2_ref[...], preferred_element_type=f32)
    if final:
        ms2 = jnp.mean(acc * acc, axis=-1, keepdims=True)
        acc = acc * lax.rsqrt(ms2 + EPS) * gf_ref[...]
    o_ref[...] = acc


def _ffn_call(x, ab, wo, g, w1, w2, gf, *, final):
    rows = x.shape[0]
    row_map = lambda i: (i, 0)
    const_map = lambda i: (0, 0)
    once = pl.Buffered(1)
    return pl.pallas_call(
        functools.partial(_ffn_kernel, final=final),
        grid=(rows // TM,),
        in_specs=[
            pl.BlockSpec((TM, D_MODEL), row_map),
            pl.BlockSpec((TM, D_A + D_B), row_map),
            pl.BlockSpec((D_A + D_B, D_MODEL), const_map, pipeline_mode=once),
            pl.BlockSpec((1, D_MODEL), const_map),
            pl.BlockSpec((D_MODEL, D_FF), const_map, pipeline_mode=once),
            pl.BlockSpec((D_FF, D_MODEL), const_map, pipeline_mode=once),
            pl.BlockSpec((1, D_MODEL), const_map),
        ],
        out_specs=pl.BlockSpec((TM, D_MODEL), row_map),
        out_shape=jax.ShapeDtypeStruct((rows, D_MODEL), jnp.float32),
        compiler_params=pltpu.CompilerParams(
            dimension_semantics=("arbitrary",), vmem_limit_bytes=VMEM_LIMIT),
        name="ffn",
    )(x, ab, wo, g, w1, w2, gf)


def _rope_tables(pos):
    posf = pos.astype(jnp.float32)[:, None]
    lane = jnp.arange(LANES)

    def one(width, rot, limit):
        half = rot // 2
        d = lane % width
        inv = ROPE_THETA ** (-jnp.arange(half, dtype=jnp.float32) * (2.0 / rot))
        ang = posf * inv[None, :]
        cos_l = jnp.take(jnp.cos(ang), d % half, axis=1)
        sin_l = jnp.take(jnp.sin(ang), d % half, axis=1)
        live = lane < limit
        lo = jnp.logical_and(d < half, live)[None, :]
        hi = jnp.logical_and(jnp.logical_and(d >= half, d < rot), live)[None, :]
        c = jnp.where(jnp.logical_or(lo, hi), cos_l, 1.0)
        return [c, jnp.where(lo, -sin_l, 0.0), jnp.where(hi, sin_l, 0.0)]

    tabs = one(HEAD_DIM, ROT_DIM, LANES) + one(IDX_DIM, IDX_ROT, LANES) + one(IDX_DIM, IDX_ROT, IDX_DIM)
    return jnp.concatenate(tabs, axis=1)


def kernel(x_prompt, x_sample, cache_k, cache_v, cache_kidx, norm_mix, w_in, gate_norm, w_spatial,
           b_spatial, w_out, norm_ffn, w_ff1, w_ff2, norm_final):
    f32, bf16 = jnp.float32, jnp.bfloat16
    depth = w_in.shape[0]
    nb_p, t_p = x_prompt.shape[:2]
    nb_s, t_s = x_sample.shape[:2]
    past = cache_k.shape[2]
    rows_p, rows_s = nb_p * t_p, nb_s * TQ
    assert t_p % TM == 0 and rows_s % TM == 0 and t_p % KT == 0
    assert t_s * 2 == TQ and t_s == CHUNK and past % CHUNK == 0
    nblk_p = t_p // TQ
    nkt_p = t_p // KT
    topk_p = min(TOPK_MAX, t_p // 4)
    topk_s = min(TOPK_MAX, (past + t_s) // 4)
    s_all = past + t_s
    s_pad = -(-s_all // KT) * KT
    nkt_s = s_pad // KT

    n_head_cols = _U0 - 88
    w_in_p = jnp.concatenate(
        [w_in[:, :, :n_head_cols], jnp.zeros((depth, D_MODEL, _U0 - n_head_cols), w_in.dtype),
         w_in[:, :, n_head_cols:]], axis=2).astype(bf16)
    w_out_b = w_out.astype(bf16)
    w1_b = w_ff1.astype(bf16)
    w2_b = w_ff2.astype(bf16)
    cm = (jnp.arange(GMLP_CHUNK)[None, :] // CHUNK <= jnp.arange(GMLP_CHUNK)[:, None] // CHUNK)
    wm_p = jnp.where(cm[None, None], w_spatial, 0.0).astype(bf16)
    small = (jnp.arange(GMLP_CHUNK) < t_s)
    wm_s = jnp.where(jnp.logical_and(small[:, None], small[None, :])[None, None], w_spatial, 0.0).astype(bf16)
    wm_p = wm_p.reshape(depth, N_GROUPS_B // 2, 2 * GMLP_CHUNK, GMLP_CHUNK)
    wm_s = wm_s.reshape(depth, N_GROUPS_B // 2, 2 * GMLP_CHUNK, GMLP_CHUNK)
    bias_sp = jnp.repeat(jnp.swapaxes(b_spatial, 1, 2), GROUP_DIM_B, axis=2)

    pos_s = past + jnp.arange(t_s, dtype=jnp.int32)
    tab_p = _rope_tables(jnp.arange(t_p, dtype=jnp.int32))
    tab_s = _rope_tables(jnp.tile(jnp.concatenate([pos_s, pos_s]), TM // TQ))

    cache_kb = cache_k.astype(bf16).reshape(depth, nb_s, past, D_KV)
    cache_vb = cache_v.astype(bf16).reshape(depth, nb_s, past, D_KV)
    cache_kib = cache_kidx.astype(bf16)
    pad_rows = jnp.zeros((nb_s, s_pad - s_all, D_KV), bf16)

    xp = x_prompt.reshape(rows_p, D_MODEL)
    xs = jnp.concatenate([x_sample, x_sample], axis=1).reshape(rows_s, D_MODEL)

    def first_half(a):
        return a.reshape(a.shape[:-2] + (nb_s, TQ, a.shape[-1]))[..., :t_s, :]

    stk_p = stk_s = None
    gvs_l = []
    for l in range(depth):
        g_mix, g_gate, g_ffn = norm_mix[l][None, :], gate_norm[l][None, :], norm_ffn[l][None, :]
        final = l == depth - 1

        qT, qiT, wiT, k_st, v_st, ki_st, kb, vT, kib, gu, gvn = _proj_call(
            xp, g_mix, w_in_p[l], tab_p, g_gate, l, depth, stk_p, bf16)
        stk_p = (k_st, v_st, ki_st)
        ab = _dsa_call(
            qT.reshape((nb_p, nblk_p) + qT.shape[1:]), qiT.reshape((nb_p, nblk_p) + qiT.shape[1:]),
            wiT.reshape((nb_p, nblk_p) + wiT.shape[1:]),
            kb.reshape(nb_p, nkt_p, KT, D_KV), vT.reshape(nb_p, nkt_p, D_KV, KT),
            kib.reshape(nb_p, nkt_p, KT, IDX_DIM),
            gvn.reshape(nb_p, nblk_p, TQ, D_B), gu.reshape(nb_p, nblk_p, TQ, D_B),
            wm_p[l], bias_sp[l], causal=True, q_pos0=0, tq_real=TQ, n_keys=t_p, topk=topk_p)
        xp = _ffn_call(xp, ab.reshape(rows_p, D_A + D_B), w_out_b[l], g_ffn, w1_b[l], w2_b[l],
                       norm_final[None, :], final=final)

        qT, qiT, wiT, k_st, v_st, ki_st, kb, vT, kib, gu, gvn = _proj_call(
            xs, g_mix, w_in_p[l], tab_s, g_gate, l, depth, stk_s, f32)
        stk_s = (k_st, v_st, ki_st)
        k_all = jnp.concatenate([cache_kb[l], first_half(kb), pad_rows], axis=1)
        v_all = jnp.concatenate([cache_vb[l], first_half(v_st[l].reshape(rows_s, D_KV)).astype(bf16), pad_rows], axis=1)
        ki_all = jnp.concatenate([cache_kib[l], first_half(kib), pad_rows[:, :, :IDX_DIM]], axis=1)
        ab = _dsa_call(
            qT.reshape((nb_s, 1) + qT.shape[1:]), qiT.reshape((nb_s, 1) + qiT.shape[1:]),
            wiT.reshape((nb_s, 1) + wiT.shape[1:]),
            k_all.reshape(nb_s, nkt_s, KT, D_KV),
            jnp.swapaxes(v_all.reshape(nb_s, nkt_s, KT, D_KV), 2, 3),
            ki_all.reshape(nb_s, nkt_s, KT, IDX_DIM),
            gvn.reshape(nb_s, 1, TQ, D_B), gu.reshape(nb_s, 1, TQ, D_B),
            wm_s[l], bias_sp[l], causal=False, q_pos0=past, tq_real=t_s, n_keys=s_all, topk=topk_s)
        xs = _ffn_call(xs, ab.reshape(rows_s, D_A + D_B), w_out_b[l], g_ffn, w1_b[l], w2_b[l],
                       norm_final[None, :], final=final)
        gvs_l.append(first_half(gvn))

    k_p, v_p, ki_p = stk_p
    k_s, v_s, ki_s = (first_half(a.reshape(depth, rows_s, -1)) for a in stk_s)
    return (xp.reshape(nb_p, t_p, D_MODEL), first_half(xs),
            k_p.reshape(depth, nb_p, t_p, N_KV_HEADS, HEAD_DIM),
            v_p.reshape(depth, nb_p, t_p, N_KV_HEADS, HEAD_DIM),
            ki_p.reshape(depth, nb_p, t_p, IDX_DIM),
            k_s.reshape(depth, nb_s, t_s, N_KV_HEADS, HEAD_DIM),
            v_s.reshape(depth, nb_s, t_s, N_KV_HEADS, HEAD_DIM),
            ki_s, jnp.stack(gvs_l))
```

```python
import functools

import numpy as np
import jax
import jax.numpy as jnp
from jax import lax
from jax.experimental import pallas as pl
from jax.experimental.pallas import tpu as pltpu

D_MODEL = 1024
CHUNK = 64
D_A = 512
D_B = 512
HEAD_DIM = 64
N_HEADS_A = 8
N_KV_HEADS = 2
GQA_GROUP = 4
D_KV = 128
ROT_DIM = 16
N_IDX_HEADS = 8
IDX_DIM = 32
IDX_ROT = 8
TOPK_MAX = 256
N_GROUPS_B = 8
GROUP_DIM_B = 64
GMLP_CHUNK = 128
D_FF = 4096
ROPE_THETA = 500000.0
EPS = 1e-6

LANES = 128
SUBLANES = 8
N_IN_PAD = 2176
TM = 512
TQ = 128
KT = 512
VMEM_LIMIT = 56 * 1024 * 1024

INT_MIN = np.int32(-2147483648)
MANT_MASK = np.int32(0x7FFFFFFF)
FAST_SEARCH_ITERS = 40
BOUND_SLACK = 1.02
L_TINY = 1e-30
NEG_BIG = -1e30
LOG2E = 1.4426950408889634

_Q0, _K0, _V0, _QI0, _KW0, _U0, _GV0 = 0, 512, 640, 768, 1024, 1152, 1664


def _gelu(x):
    return x * (0.5 * (1.0 + jnp.tanh(0.7978845608028654 * (x + 0.044715 * (x * x * x)))))


def _rope128(x, c, s_lo, s_hi, half):
    return x * c + pltpu.roll(x, LANES - half, 1) * s_lo + pltpu.roll(x, half, 1) * s_hi


def _sortable(bits):
    return bits ^ ((bits >> 31) & MANT_MASK)


def _proj_kernel(*refs, n_alias):
    (x_ref, g_ref, w_ref, tab_ref, gn_ref) = refs[:5]
    (qT_ref, qiT_ref, wi_ref, k_ref, v_ref, ki_ref, kb_ref, vT_ref, kib_ref, gu_ref, gvn_ref) = refs[5 + n_alias:]
    bf16 = jnp.bfloat16
    x = x_ref[...]
    ms = jnp.mean(x * x, axis=-1, keepdims=True)
    xn = (x * lax.rsqrt(ms + EPS) * g_ref[...]).astype(bf16)
    z = jnp.dot(xn, w_ref[...], preferred_element_type=jnp.float32)

    def tab(j):
        return tab_ref[:, j * LANES:(j + 1) * LANES]

    cq, sq_lo, sq_hi = tab(0), tab(1), tab(2)
    ci, si_lo, si_hi = tab(3), tab(4), tab(5)
    ck, sk_lo, sk_hi = tab(6), tab(7), tab(8)
    n_blk = TM // TQ

    def blocks_T(slab):
        return [slab[b * TQ:(b + 1) * TQ, :].T for b in range(n_blk)]

    qT_ref[...] = jnp.zeros(qT_ref.shape, qT_ref.dtype)
    for j in range(D_A // LANES):
        zq = _rope128(z[:, _Q0 + j * LANES:_Q0 + (j + 1) * LANES], cq, sq_lo, sq_hi, ROT_DIM // 2)
        zq = zq * (HEAD_DIM ** -0.5 * LOG2E)
        for b, t in enumerate(blocks_T(zq)):
            for r in range(LANES // HEAD_DIM):
                h = j * (LANES // HEAD_DIM) + r
                g = h // GQA_GROUP
                qT_ref[b, g * HEAD_DIM:(g + 1) * HEAD_DIM, h * TQ:(h + 1) * TQ] = (
                    t[r * HEAD_DIM:(r + 1) * HEAD_DIM, :].astype(bf16))

    kk = _rope128(z[:, _K0:_K0 + LANES], cq, sq_lo, sq_hi, ROT_DIM // 2)
    for g in range(N_KV_HEADS):
        k_ref[:, g, :] = kk[:, g * HEAD_DIM:(g + 1) * HEAD_DIM]
    kb_ref[...] = kk.astype(bf16)
    vv = z[:, _V0:_V0 + LANES]
    for g in range(N_KV_HEADS):
        v_ref[:, g, :] = vv[:, g * HEAD_DIM:(g + 1) * HEAD_DIM]
    for b, t in enumerate(blocks_T(vv)):
        kt, off = (b * TQ) // KT, (b * TQ) % KT
        vT_ref[kt, :, off:off + TQ] = t.astype(bf16)

    for j in range(N_IDX_HEADS * IDX_DIM // LANES):
        zi = _rope128(z[:, _QI0 + j * LANES:_QI0 + (j + 1) * LANES], ci, si_lo, si_hi, IDX_ROT // 2)
        for b, t in enumerate(blocks_T(zi)):
            for r in range(LANES // IDX_DIM):
                h = j * (LANES // IDX_DIM) + r
                qiT_ref[b, :, h * TQ:(h + 1) * TQ] = t[r * IDX_DIM:(r + 1) * IDX_DIM, :].astype(bf16)

    kw = _rope128(z[:, _KW0:_KW0 + LANES], ck, sk_lo, sk_hi, IDX_ROT // 2)
    ki_ref[...] = kw[:, :IDX_DIM]
    kib_ref[...] = kw[:, :IDX_DIM].astype(bf16)
    for b, t in enumerate(blocks_T(kw)):
        wi_ref[b] = t[IDX_DIM:IDX_DIM + N_IDX_HEADS, :] * (IDX_DIM ** -0.5 * N_IDX_HEADS ** -0.5)

    gu_ref[...] = _gelu(z[:, _U0:_U0 + D_B]).astype(gu_ref.dtype)
    gg = _gelu(z[:, _GV0:_GV0 + D_B])
    gms = jnp.mean(gg * gg, axis=-1, keepdims=True)
    gvn_ref[...] = (gg * lax.rsqrt(gms + EPS) * gn_ref[...]).astype(gvn_ref.dtype)


def _proj_call(x, g, w, tab, gn, layer, depth, stacks, gvn_dtype):
    rows = x.shape[0]
    n_tiles = rows // TM
    n_pos_tiles = tab.shape[0] // TM
    n_blk = TM // TQ
    f32, bf16 = jnp.float32, jnp.bfloat16
    n_alias = len(stacks)

    row2 = lambda i: (i, 0)
    const2 = lambda i: (0, 0)
    lead3 = lambda i: (i, 0, 0)
    lay3 = lambda i: (layer, i, 0)
    lay4 = lambda i: (layer, i, 0, 0)
    outs = [
        ((rows // TQ, D_KV, N_HEADS_A * TQ), bf16, (n_blk, D_KV, N_HEADS_A * TQ), lead3),
        ((rows // TQ, IDX_DIM, N_IDX_HEADS * TQ), bf16, (n_blk, IDX_DIM, N_IDX_HEADS * TQ), lead3),
        ((rows // TQ, N_IDX_HEADS, TQ), f32, (n_blk, N_IDX_HEADS, TQ), lead3),
        ((depth, rows, N_KV_HEADS, HEAD_DIM), f32, (None, TM, N_KV_HEADS, HEAD_DIM), lay4),
        ((depth, rows, N_KV_HEADS, HEAD_DIM), f32, (None, TM, N_KV_HEADS, HEAD_DIM), lay4),
        ((depth, rows, IDX_DIM), f32, (None, TM, IDX_DIM), lay3),
        ((rows, D_KV), bf16, (TM, D_KV), row2),
        ((rows // KT, D_KV, KT), bf16, (TM // KT, D_KV, KT), lead3),
        ((rows, IDX_DIM), bf16, (TM, IDX_DIM), row2),
        ((rows, D_B), bf16, (TM, D_B), row2),
        ((rows, D_B), gvn_dtype, (TM, D_B), row2),
    ]
    in_specs = [
        pl.BlockSpec((TM, D_MODEL), row2),
        pl.BlockSpec((1, D_MODEL), const2),
        pl.BlockSpec((D_MODEL, N_IN_PAD), const2),
        pl.BlockSpec((TM, 9 * LANES), lambda i: (i % n_pos_tiles, 0)),
        pl.BlockSpec((1, D_B), const2),
    ] + [pl.BlockSpec(memory_space=pl.ANY)] * n_alias
    args = (x, g, w, tab, gn) + tuple(stacks)
    return pl.pallas_call(
        functools.partial(_proj_kernel, n_alias=n_alias),
        grid=(n_tiles,),
        in_specs=in_specs,
        out_specs=tuple(pl.BlockSpec(blk, imap) for _, _, blk, imap in outs),
        out_shape=tuple(jax.ShapeDtypeStruct(shape, dt) for shape, dt, _, _ in outs),
        input_output_aliases={5 + j: 3 + j for j in range(n_alias)},
        compiler_params=pltpu.CompilerParams(
            dimension_semantics=("arbitrary",), vmem_limit_bytes=VMEM_LIMIT),
        name="proj",
    )(*args)


def _dsa_kernel(qT_ref, qiT_ref, wi_ref, k_ref, vT_ref, ki_ref, gvn_ref, gu_ref, wm_ref, bias_ref,
                ab_ref, skey_ref, sel_ref, kn_ref, *, causal, q_pos0, tq_real, n_keys, n_kt, topk):
    f32, bf16, i32 = jnp.float32, jnp.bfloat16, jnp.int32
    i = pl.program_id(1)
    if causal:
        n_tiles = ((i + 1) * TQ + KT - 1) // KT
        q0 = i * TQ
    else:
        n_tiles = n_kt
        q0 = q_pos0
    lane = lax.broadcasted_iota(i32, (1, TQ), 1)
    qchunk = (q0 + (lane & (tq_real - 1))) // CHUNK
    n_adm = jnp.minimum((qchunk + 1) * CHUNK, n_keys)

    def fold8(a, op):
        return op(a.reshape(a.shape[0] // SUBLANES, SUBLANES, a.shape[1]), axis=0)

    def score_tile(t, carry, partial):
        kmax8, kmin8, nz8, np8 = carry
        ki_t = ki_ref[t]
        r = jnp.maximum(jnp.dot(ki_t, qiT_ref[...], preferred_element_type=f32), 0.0)
        acc = r[:, :TQ] * wi_ref[0:1, :]
        for h in range(1, N_IDX_HEADS):
            acc = acc + r[:, h * TQ:(h + 1) * TQ] * wi_ref[h:h + 1, :]
        acc = jnp.where(acc == 0.0, 0.0, acc)
        key = _sortable(pltpu.bitcast(acc, i32))
        if partial:
            kpos = t * KT + lax.broadcasted_iota(i32, (KT, TQ), 0)
            adm = (kpos // CHUNK) <= qchunk
            mkey = jnp.where(adm, key, INT_MIN)
            lowkey = jnp.where(adm, key, ~INT_MIN)
        else:
            mkey = lowkey = key
        skey_ref[t] = mkey
        kmax8 = jnp.maximum(kmax8, fold8(mkey, jnp.max))
        kmin8 = jnp.minimum(kmin8, fold8(lowkey, jnp.min))
        nz8 = nz8 + fold8(jnp.where(mkey >= 0, 1, 0).astype(i32), jnp.sum)
        np8 = np8 + fold8(jnp.where(mkey >= 1, 1, 0).astype(i32), jnp.sum)
        return kmax8, kmin8, nz8, np8

    n_whole = (i * TQ) // KT if causal else n_keys // KT
    zero8 = jnp.zeros((SUBLANES, TQ), i32)
    carry = (jnp.full((SUBLANES, TQ), INT_MIN, i32), jnp.full((SUBLANES, TQ), ~INT_MIN, i32), zero8, zero8)
    carry = lax.fori_loop(0, n_whole, functools.partial(score_tile, partial=False), carry)
    kmax8, kmin8, nz8, np8 = lax.fori_loop(n_whole, n_tiles, functools.partial(score_tile, partial=True), carry)
    kmax = jnp.max(kmax8, axis=0, keepdims=True)
    kmin = jnp.min(kmin8, axis=0, keepdims=True)
    n_ge0 = jnp.sum(nz8, axis=0, keepdims=True)
    n_gt0 = jnp.sum(np8, axis=0, keepdims=True)

    def count(pred_fn):
        def body(t, acc8):
            return acc8 + fold8(jnp.where(pred_fn(skey_ref[t]), 1, 0).astype(i32), jnp.sum)
        acc8 = lax.fori_loop(0, n_tiles, body, jnp.zeros((SUBLANES, TQ), i32))
        return jnp.sum(acc8, axis=0, keepdims=True)

    few = n_adm <= topk
    tie0 = jnp.logical_and(n_gt0 < topk, n_ge0 >= topk)
    hit1 = n_gt0 == topk
    above = n_gt0 > topk

    def n_open(done):
        return jnp.max(jnp.where(done == 0, 1.0, 0.0))

    def close_adjacent(lo, hi, a, thr, cge, done):
        adj = jnp.logical_and(lo + 1 == hi, done == 0)
        return jnp.where(adj, lo, thr), jnp.where(adj, a, cge), jnp.where(adj, 1, done)

    def search_step(core, it):
        lo, hi, a, b, thr, cge, done = core
        thr, cge, done = close_adjacent(lo, hi, a, thr, cge, done)
        lo_v = pltpu.bitcast(_sortable(lo), f32)
        hi_v = pltpu.bitcast(_sortable(hi), f32)
        af, bf = a.astype(f32), b.astype(f32)
        lin = (af - topk) / (af - bf)
        la = jnp.log(af + 0.5)
        lg = (la - np.float32(np.log(topk))) / (la - jnp.log(bf + 0.5))
        frac = jnp.where(af > 4.0 * jnp.maximum(bf, 1.0), lg, lin)
        c_int = _sortable(pltpu.bitcast(lo_v + (hi_v - lo_v) * frac, i32))
        c_mid = (lo >> 1) + (hi >> 1) + (lo & hi & 1)
        c = jnp.where((it & 1) == 0, c_int, c_mid)
        c = jnp.minimum(jnp.maximum(c, lo + 1), hi - 1)
        c = jnp.where(done == 1, thr, c)
        cnt = count(lambda tile: tile >= c)
        live = done == 0
        found = jnp.logical_and(live, cnt == topk)
        up = jnp.logical_and(live, cnt > topk)
        dn = jnp.logical_and(live, cnt < topk)
        return (jnp.where(up, c, lo), jnp.where(dn, c, hi), jnp.where(up, cnt, a), jnp.where(dn, cnt, b),
                jnp.where(found, c, thr), jnp.where(found, cnt, cge), jnp.where(found, 1, done))

    def search_cond(st):
        return jnp.logical_and(st[2] > 0, st[1] < FAST_SEARCH_ITERS)

    def search_body(st):
        core, it, _ = st
        core = search_step(search_step(core, it), it + 1)
        return core, it + 2, n_open(core[6])

    zero = jnp.zeros((1, TQ), i32)
    done0 = jnp.where(jnp.logical_or(few, jnp.logical_or(tie0, hit1)), 1, 0)
    thr0 = jnp.where(few, INT_MIN + 1, jnp.where(tie0, 0, 1))
    cge0 = jnp.where(few, 0, jnp.where(tie0, n_ge0, topk))
    core0 = (jnp.where(above, 1, kmin), jnp.where(above, kmax + 1, 0),
             jnp.where(above, n_gt0, n_adm), jnp.where(above, 0, n_ge0), thr0, cge0, done0)
    (lo, hi, a, _, thr, cge, done), _, _ = lax.while_loop(
        search_cond, search_body, (core0, jnp.int32(0), n_open(done0)))
    thr, cge, done = close_adjacent(lo, hi, a, thr, cge, done)
    sel_ref[0:1, :] = thr
    sel_ref[1:2, :] = cge

    @pl.when(n_open(done) > 0)
    def _():
        def bit_body(j, bst):
            t_u, cnt_at = bst
            c_u = t_u | jnp.left_shift(jnp.int32(1), 31 - j)
            cnt = count(lambda tile: tile >= (c_u ^ INT_MIN))
            ok = cnt >= topk
            return jnp.where(ok, c_u, t_u), jnp.where(ok, cnt, cnt_at)

        t_u, cnt_at = lax.fori_loop(0, 32, bit_body, (zero, zero))
        sel_ref[0:1, :] = jnp.where(done == 1, thr, jnp.maximum(t_u ^ INT_MIN, INT_MIN + 1))
        sel_ref[1:2, :] = jnp.where(done == 1, cge, cnt_at)

    thr = sel_ref[0:1, :]
    cge = sel_ref[1:2, :]

    @pl.when(jnp.max(jnp.where(cge > topk, 1.0, 0.0)) > 0)
    def _():
        need = (topk - count(lambda tile: tile > thr)).astype(f32)
        rr = lax.broadcasted_iota(i32, (KT, KT), 0)
        cc = lax.broadcasted_iota(i32, (KT, KT), 1)
        tri = jnp.where(cc <= rr, 1.0, 0.0).astype(bf16)

        def body(t, seen):
            tile = skey_ref[t]
            eq = tile == thr
            eqf = jnp.where(eq, 1.0, 0.0).astype(bf16)
            rank = jnp.dot(tri, eqf, preferred_element_type=f32) + seen
            skey_ref[t] = jnp.where(jnp.logical_and(eq, rank > need), INT_MIN, tile)
            return rank[KT - 1:KT, :]

        lax.fori_loop(0, n_tiles, body, jnp.zeros((1, TQ), f32))

    n_hq = N_HEADS_A * TQ
    g_hq = GQA_GROUP * TQ

    @pl.when(i == 0)
    def _():
        def kn_tile(t, carry):
            kf = k_ref[t].astype(f32)
            sq = kf * kf
            return tuple(
                jnp.maximum(carry[g], jnp.max(jnp.sum(sq[:, g * HEAD_DIM:(g + 1) * HEAD_DIM], axis=1, keepdims=True),
                                              axis=0, keepdims=True))
                for g in range(N_KV_HEADS))

        kn = lax.fori_loop(0, n_kt, kn_tile, tuple(jnp.zeros((1, 1), f32) for _ in range(N_KV_HEADS)))
        kn_ref[0:1, :] = jnp.concatenate([jnp.broadcast_to(kn[g], (1, g_hq)) for g in range(N_KV_HEADS)], axis=1)

    def logits(t):
        bias = jnp.where(skey_ref[t] >= thr, 0.0, -jnp.inf)
        bias = jnp.concatenate([bias] * N_HEADS_A, axis=1)
        return jnp.dot(k_ref[t], qT_ref[...], preferred_element_type=f32) + bias

    def attend(mrow):
        ones_rows = jnp.ones((SUBLANES, KT), bf16)

        def pv_tile(t, st):
            prb = jnp.exp2(logits(t) - mrow).astype(bf16)
            out = []
            for g in range(N_KV_HEADS):
                lhs = jnp.concatenate([vT_ref[t, g * HEAD_DIM:(g + 1) * HEAD_DIM, :], ones_rows], axis=0)
                out.append(st[g] + jnp.dot(lhs, prb[:, g * g_hq:(g + 1) * g_hq], preferred_element_type=f32))
            return tuple(out)

        acc = lax.fori_loop(0, n_tiles, pv_tile,
                            tuple(jnp.zeros((HEAD_DIM + SUBLANES, g_hq), f32) for _ in range(N_KV_HEADS)))
        acc0, acc1 = acc[0][:HEAD_DIM], acc[1][:HEAD_DIM]
        l8 = jnp.concatenate([acc[0][HEAD_DIM:], acc[1][HEAD_DIM:]], axis=1)
        lsum = l8[0:1, :]
        linv = 1.0 / lsum
        oT = [acc0 * linv[:, :g_hq], acc1 * linv[:, g_hq:]]
        for j in range(D_A // LANES):
            g, hh = (2 * j) // GQA_GROUP, (2 * j) % GQA_GROUP
            pair = oT[g][:, hh * TQ:(hh + 2) * TQ]
            stacked = jnp.concatenate([pair[:, :TQ], pair[:, TQ:]], axis=0)
            ab_ref[:, j * LANES:(j + 1) * LANES] = stacked.T.astype(ab_ref.dtype)
        return jnp.min(lsum)

    qf = qT_ref[...].astype(f32)
    lmin = attend(jnp.sqrt(jnp.sum(qf * qf, axis=0, keepdims=True) * kn_ref[0:1, :]) * BOUND_SLACK)

    @pl.when(jnp.logical_not(lmin >= L_TINY))
    def _():
        m8 = lax.fori_loop(0, n_tiles, lambda t, m: jnp.maximum(m, fold8(logits(t), jnp.max)),
                           jnp.full((SUBLANES, n_hq), -jnp.inf, f32))
        attend(jnp.maximum(jnp.max(m8, axis=0, keepdims=True), NEG_BIG))

    lane_c = lax.broadcasted_iota(i32, (GMLP_CHUNK, LANES), 1)
    for p2 in range(N_GROUPS_B // 2):
        xg = gvn_ref[:, p2 * LANES:(p2 + 1) * LANES].astype(bf16)
        both = jnp.dot(wm_ref[p2], xg, preferred_element_type=f32)
        mix = jnp.where(lane_c < GROUP_DIM_B, both[:GMLP_CHUNK], both[GMLP_CHUNK:])
        mix = mix + bias_ref[:, p2 * LANES:(p2 + 1) * LANES]
        gu = gu_ref[:, p2 * LANES:(p2 + 1) * LANES].astype(f32)
        ab_ref[:, D_A + p2 * LANES:D_A + (p2 + 1) * LANES] = (gu * mix).astype(ab_ref.dtype)


def _dsa_call(qT, qiT, wi, k, vT, ki, gvn, gu, wm, bias, *, causal, q_pos0, tq_real, n_keys, topk):
    nbatch, nblk = qT.shape[:2]
    n_kt = k.shape[1]
    kern = functools.partial(_dsa_kernel, causal=causal, q_pos0=q_pos0, tq_real=tq_real,
                             n_keys=n_keys, n_kt=n_kt, topk=topk)
    blk4 = lambda b, i: (b, i, 0, 0)
    per_b = lambda b, i: (b, 0, 0, 0)
    return pl.pallas_call(
        kern,
        grid=(nbatch, nblk),
        in_specs=[
            pl.BlockSpec((None, None, D_KV, N_HEADS_A * TQ), blk4),
            pl.BlockSpec((None, None, IDX_DIM, N_IDX_HEADS * TQ), blk4),
            pl.BlockSpec((None, None, N_IDX_HEADS, TQ), blk4),
            pl.BlockSpec((None, n_kt, KT, D_KV), per_b),
            pl.BlockSpec((None, n_kt, D_KV, KT), per_b),
            pl.BlockSpec((None, n_kt, KT, IDX_DIM), per_b),
            pl.BlockSpec((None, None, TQ, D_B), blk4),
            pl.BlockSpec((None, None, TQ, D_B), blk4),
            pl.BlockSpec((N_GROUPS_B // 2, 2 * GMLP_CHUNK, GMLP_CHUNK), lambda b, i: (0, 0, 0)),
            pl.BlockSpec((GMLP_CHUNK, D_B), lambda b, i: (0, 0)),
        ],
        out_specs=pl.BlockSpec((None, None, TQ, D_A + D_B), blk4),
        out_shape=jax.ShapeDtypeStruct((nbatch, nblk, TQ, D_A + D_B), jnp.bfloat16),
        scratch_shapes=[pltpu.VMEM((n_kt, KT, TQ), jnp.int32),
                        pltpu.VMEM((SUBLANES, TQ), jnp.int32),
                        pltpu.VMEM((SUBLANES, N_HEADS_A * TQ), jnp.float32)],
        compiler_params=pltpu.CompilerParams(
            dimension_semantics=("arbitrary", "arbitrary"), vmem_limit_bytes=VMEM_LIMIT),
        name="dsa_prompt" if causal else "dsa_sample",
    )(qT, qiT, wi, k, vT, ki, gvn, gu, wm, bias)


def _ffn_kernel(x_ref, ab_ref, wo_ref, g_ref, w1_ref, w2_ref, gf_ref, o_ref, *, final):
    f32 = jnp.float32
    x = x_ref[...] + jnp.dot(ab_ref[...], wo_ref[...], preferred_element_type=f32)
    ms = jnp.mean(x * x, axis=-1, keepdims=True)
    xn = (x * lax.rsqrt(ms + EPS) * g_ref[...]).astype(jnp.bfloat16)
    h = jnp.maximum(jnp.dot(xn, w1_ref[...], preferred_element_type=f32), 0.0)
    acc = x + jnp.dot((h * h).astype(jnp.bfloat16), w2_ref[...], preferred_element_type=f32)
    if final:
        ms2 = jnp.mean(acc * acc, axis=-1, keepdims=True)
        acc = acc * lax.rsqrt(ms2 + EPS) * gf_ref[...]
    o_ref[...] = acc


def _ffn_call(x, ab, wo, g, w1, w2, gf, *, final):
    rows = x.shape[0]
    row_map = lambda i: (i, 0)
    const_map = lambda i: (0, 0)
    once = pl.Buffered(1)
    return pl.pallas_call(
        functools.partial(_ffn_kernel, final=final),
        grid=(rows // TM,),
        in_specs=[
            pl.BlockSpec((TM, D_MODEL), row_map),
            pl.BlockSpec((TM, D_A + D_B), row_map),
            pl.BlockSpec((D_A + D_B, D_MODEL), const_map, pipeline_mode=once),
            pl.BlockSpec((1, D_MODEL), const_map),
            pl.BlockSpec((D_MODEL, D_FF), const_map, pipeline_mode=once),
            pl.BlockSpec((D_FF, D_MODEL), const_map, pipeline_mode=once),
            pl.BlockSpec((1, D_MODEL), const_map),
        ],
        out_specs=pl.BlockSpec((TM, D_MODEL), row_map),
        out_shape=jax.ShapeDtypeStruct((rows, D_MODEL), jnp.float32),
        compiler_params=pltpu.CompilerParams(
            dimension_semantics=("arbitrary",), vmem_limit_bytes=VMEM_LIMIT),
        name="ffn",
    )(x, ab, wo, g, w1, w2, gf)


def _rope_tables(pos):
    posf = pos.astype(jnp.float32)[:, None]
    lane = jnp.arange(LANES)

    def one(width, rot, limit):
        half = rot // 2
        d = lane % width
        inv = ROPE_THETA ** (-jnp.arange(half, dtype=jnp.float32) * (2.0 / rot))
        ang = posf * inv[None, :]
        cos_l = jnp.take(jnp.cos(ang), d % half, axis=1)
        sin_l = jnp.take(jnp.sin(ang), d % half, axis=1)
        live = lane < limit
        lo = jnp.logical_and(d < half, live)[None, :]
        hi = jnp.logical_and(jnp.logical_and(d >= half, d < rot), live)[None, :]
        c = jnp.where(jnp.logical_or(lo, hi), cos_l, 1.0)
        return [c, jnp.where(lo, -sin_l, 0.0), jnp.where(hi, sin_l, 0.0)]

    tabs = one(HEAD_DIM, ROT_DIM, LANES) + one(IDX_DIM, IDX_ROT, LANES) + one(IDX_DIM, IDX_ROT, IDX_DIM)
    return jnp.concatenate(tabs, axis=1)


def kernel(x_prompt, x_sample, cache_k, cache_v, cache_kidx, norm_mix, w_in, gate_norm, w_spatial,
           b_spatial, w_out, norm_ffn, w_ff1, w_ff2, norm_final):
    f32, bf16 = jnp.float32, jnp.bfloat16
    depth = w_in.shape[0]
    nb_p, t_p = x_prompt.shape[:2]
    nb_s, t_s = x_sample.shape[:2]
    past = cache_k.shape[2]
    rows_p, rows_s = nb_p * t_p, nb_s * TQ
    assert t_p % TM == 0 and rows_s % TM == 0 and t_p % KT == 0
    assert t_s * 2 == TQ and t_s == CHUNK and past % CHUNK == 0
    nblk_p = t_p // TQ
    nkt_p = t_p // KT
    topk_p = min(TOPK_MAX, t_p // 4)
    topk_s = min(TOPK_MAX, (past + t_s) // 4)
    s_all = past + t_s
    s_pad = -(-s_all // KT) * KT
    nkt_s = s_pad // KT

    n_head_cols = _U0 - 88
    w_in_p = jnp.concatenate(
        [w_in[:, :, :n_head_cols], jnp.zeros((depth, D_MODEL, _U0 - n_head_cols), w_in.dtype),
         w_in[:, :, n_head_cols:]], axis=2).astype(bf16)
    w_out_b = w_out.astype(bf16)
    w1_b = w_ff1.astype(bf16)
    w2_b = w_ff2.astype(bf16)
    cm = (jnp.arange(GMLP_CHUNK)[None, :] // CHUNK <= jnp.arange(GMLP_CHUNK)[:, None] // CHUNK)
    wm_p = jnp.where(cm[None, None], w_spatial, 0.0).astype(bf16)
    small = (jnp.arange(GMLP_CHUNK) < t_s)
    wm_s = jnp.where(jnp.logical_and(small[:, None], small[None, :])[None, None], w_spatial, 0.0).astype(bf16)
    wm_p = wm_p.reshape(depth, N_GROUPS_B // 2, 2 * GMLP_CHUNK, GMLP_CHUNK)
    wm_s = wm_s.reshape(depth, N_GROUPS_B // 2, 2 * GMLP_CHUNK, GMLP_CHUNK)
    bias_sp = jnp.repeat(jnp.swapaxes(b_spatial, 1, 2), GROUP_DIM_B, axis=2)

    pos_s = past + jnp.arange(t_s, dtype=jnp.int32)
    tab_p = _rope_tables(jnp.arange(t_p, dtype=jnp.int32))
    tab_s = _rope_tables(jnp.tile(jnp.concatenate([pos_s, pos_s]), TM // TQ))

    cache_kb = cache_k.astype(bf16).reshape(depth, nb_s, past, D_KV)
    cache_vb = cache_v.astype(bf16).reshape(depth, nb_s, past, D_KV)
    cache_kib = cache_kidx.astype(bf16)
    pad_rows = jnp.zeros((nb_s, s_pad - s_all, D_KV), bf16)

    xp = x_prompt.reshape(rows_p, D_MODEL)
    xs = jnp.concatenate([x_sample, x_sample], axis=1).reshape(rows_s, D_MODEL)

    def first_half(a):
        return a.reshape(a.shape[:-2] + (nb_s, TQ, a.shape[-1]))[..., :t_s, :]

    def empty_stacks(rows):
        return (jnp.zeros((depth, rows, N_KV_HEADS, HEAD_DIM), f32), jnp.zeros((depth, rows, N_KV_HEADS, HEAD_DIM), f32),
                jnp.zeros((depth, rows, IDX_DIM), f32))

    stk_p, stk_s = empty_stacks(rows_p), empty_stacks(rows_s)
    gvs_l = []
    for l in range(depth):
        g_mix, g_gate, g_ffn = norm_mix[l][None, :], gate_norm[l][None, :], norm_ffn[l][None, :]
        final = l == depth - 1

        qT, qiT, wiT, k_st, v_st, ki_st, kb, vT, kib, gu, gvn = _proj_call(
            xp, g_mix, w_in_p[l], tab_p, g_gate, l, depth, stk_p, bf16)
        stk_p = (k_st, v_st, ki_st)
        ab = _dsa_call(
            qT.reshape((nb_p, nblk_p) + qT.shape[1:]), qiT.reshape((nb_p, nblk_p) + qiT.shape[1:]),
            wiT.reshape((nb_p, nblk_p) + wiT.shape[1:]),
            kb.reshape(nb_p, nkt_p, KT, D_KV), vT.reshape(nb_p, nkt_p, D_KV, KT),
            kib.reshape(nb_p, nkt_p, KT, IDX_DIM),
            gvn.reshape(nb_p, nblk_p, TQ, D_B), gu.reshape(nb_p, nblk_p, TQ, D_B),
            wm_p[l], bias_sp[l], causal=True, q_pos0=0, tq_real=TQ, n_keys=t_p, topk=topk_p)
        xp = _ffn_call(xp, ab.reshape(rows_p, D_A + D_B), w_out_b[l], g_ffn, w1_b[l], w2_b[l],
                       norm_final[None, :], final=final)

        qT, qiT, wiT, k_st, v_st, ki_st, kb, vT, kib, gu, gvn = _proj_call(
            xs, g_mix, w_in_p[l], tab_s, g_gate, l, depth, stk_s, f32)
        stk_s = (k_st, v_st, ki_st)
        k_all = jnp.concatenate([cache_kb[l], first_half(kb), pad_rows], axis=1)
        v_all = jnp.concatenate([cache_vb[l], first_half(v_st[l].reshape(rows_s, D_KV)).astype(bf16), pad_rows], axis=1)
        ki_all = jnp.concatenate([cache_kib[l], first_half(kib), pad_rows[:, :, :IDX_DIM]], axis=1)
        ab = _dsa_call(
            qT.reshape((nb_s, 1) + qT.shape[1:]), qiT.reshape((nb_s, 1) + qiT.shape[1:]),
            wiT.reshape((nb_s, 1) + wiT.shape[1:]),
            k_all.reshape(nb_s, nkt_s, KT, D_KV),
            jnp.swapaxes(v_all.reshape(nb_s, nkt_s, KT, D_KV), 2, 3),
            ki_all.reshape(nb_s, nkt_s, KT, IDX_DIM),
            gvn.reshape(nb_s, 1, TQ, D_B), gu.reshape(nb_s, 1, TQ, D_B),
            wm_s[l], bias_sp[l], causal=False, q_pos0=past, tq_real=t_s, n_keys=s_all, topk=topk_s)
        xs = _ffn_call(xs, ab.reshape(rows_s, D_A + D_B), w_out_b[l], g_ffn, w1_b[l], w2_b[l],
                       norm_final[None, :], final=final)
        gvs_l.append(first_half(gvn))

    k_p, v_p, ki_p = stk_p
    k_s, v_s, ki_s = (first_half(a.reshape(depth, rows_s, -1)) for a in stk_s)
    return (xp.reshape(nb_p, t_p, D_MODEL), first_half(xs),
            k_p.reshape(depth, nb_p, t_p, N_KV_HEADS, HEAD_DIM),
            v_p.reshape(depth, nb_p, t_p, N_KV_HEADS, HEAD_DIM),
            ki_p.reshape(depth, nb_p, t_p, IDX_DIM),
            k_s.reshape(depth, nb_s, t_s, N_KV_HEADS, HEAD_DIM),
            v_s.reshape(depth, nb_s, t_s, N_KV_HEADS, HEAD_DIM),
            ki_s, jnp.stack(gvs_l))
```
